```python
import math, functools
import jax, jax.numpy as jnp
from jax import lax
import numpy as np

D_MODEL = 1024
BATCH = 2
SEQ = 8192
DEPTH = 1
DEC_BATCH = 128
DEC_SEQ = 8
PAST_LEN = 2048
PAGE_SIZE = 128

D_MIX = D_MODEL
D_CONV = D_MIX // 2
CONV_W = 3
N_HEADS = 8
HEAD_DIM = (D_MIX - D_CONV) // N_HEADS
N_IDX_HEADS = 8
IDX_DIM = 64
TOPK_MAX = 256
Q_BLOCK = 128
ROPE_THETA = 10000.0
N_MEM = 256
MEM_HEADS = 4
MEM_HEAD_DIM = D_MODEL // MEM_HEADS
D_FF = 2816
EPS = 1e-6
ATTN_SCALE = HEAD_DIM ** -0.5
MEM_SCALE = MEM_HEAD_DIM ** -0.5
IDX_SCALE = (N_IDX_HEADS * IDX_DIM) ** -0.5
PROJ_SIZES = (D_CONV, D_CONV, D_CONV, N_HEADS * HEAD_DIM, N_HEADS * HEAD_DIM, N_HEADS * HEAD_DIM, N_IDX_HEADS * IDX_DIM, IDX_DIM, N_IDX_HEADS)
D_IN = 3 * D_CONV + 3 * N_HEADS * HEAD_DIM + N_IDX_HEADS * IDX_DIM + IDX_DIM + N_IDX_HEADS

kernel_name = 'hybrid_conv_dsa_memory_convffn_step'


def rms_norm(x, g):
    xf = x.astype(jnp.float32)
    y = xf * lax.rsqrt(jnp.mean(xf * xf, axis=-1, keepdims=True) + EPS)
    return (y * g.astype(jnp.float32)).astype(x.dtype)


def rope(x, pos):
    d = x.shape[-1]
    half = d // 2
    inv = jnp.exp(jnp.arange(half, dtype=jnp.float32) * (-2.0 * math.log(ROPE_THETA) / d))
    ang = pos.astype(jnp.float32)[:, None] * inv[None, :]
    shape = (ang.shape[0],) + (1,) * (x.ndim - 3) + (half,)
    cos = jnp.cos(ang).reshape(shape)
    sin = jnp.sin(ang).reshape(shape)
    xf = x.astype(jnp.float32)
    x1, x2 = xf[..., :half], xf[..., half:]
    return jnp.concatenate([x1 * cos - x2 * sin, x2 * cos + x1 * sin], axis=-1).astype(x.dtype)


def causal_dwconv3(u, prev, w):
    t = u.shape[1]
    ext = jnp.concatenate([prev.astype(u.dtype), u], axis=1)
    y = sum(ext[:, j:j + t] * w[j] for j in range(CONV_W))
    return y, ext[:, t:]


def split_projection(z, pos, q_g, k_g):
    b, t = z.shape[:2]
    cuts = np.cumsum(PROJ_SIZES)[:-1].tolist()
    cb, cc, ch, q, k, v, qi, ki, iw = jnp.split(z, cuts, axis=-1)
    q = rope(rms_norm(q.reshape(b, t, N_HEADS, HEAD_DIM), q_g), pos)
    k = rope(rms_norm(k.reshape(b, t, N_HEADS, HEAD_DIM), k_g), pos)
    v = v.reshape(b, t, N_HEADS, HEAD_DIM)
    qi = rope(qi.reshape(b, t, N_IDX_HEADS, IDX_DIM), pos)
    ki = rope(ki, pos)
    return cb, cc, ch, q, k, v, qi, ki, iw


def indexer_scores(qi, iw, ki):
    dots = jnp.einsum('bthd,bsd->bths', qi, ki, preferred_element_type=jnp.float32)
    return jnp.einsum('bths,bth->bts', jax.nn.relu(dots), iw.astype(jnp.float32)) * IDX_SCALE


def prompt_sparse_attention(q, k, v, qi, ki, iw):
    b, s = q.shape[:2]
    topk = min(TOPK_MAX, s // 4)
    key_pos = jnp.arange(s)
    gather_rows = jax.vmap(lambda rows, idx: rows[idx])

    def block(i):
        t0 = i * Q_BLOCK
        qb = lax.dynamic_slice_in_dim(q, t0, Q_BLOCK, axis=1)
        qib = lax.dynamic_slice_in_dim(qi, t0, Q_BLOCK, axis=1)
        iwb = lax.dynamic_slice_in_dim(iw, t0, Q_BLOCK, axis=1)
        q_pos = t0 + jnp.arange(Q_BLOCK)
        sc = indexer_scores(qib, iwb, ki)
        sc = jnp.where(key_pos[None, None, :] <= q_pos[None, :, None], sc, -jnp.inf)
        _, sel = lax.top_k(sc, topk)
        valid = sel <= q_pos[None, :, None]
        ks = gather_rows(k, sel)
        vs = gather_rows(v, sel)
        logits = jnp.einsum('bthd,btkhd->bthk', qb, ks, preferred_element_type=jnp.float32) * ATTN_SCALE
        p = jax.nn.softmax(jnp.where(valid[:, :, None, :], logits, -jnp.inf), axis=-1)
        o = jnp.einsum('bthk,btkhd->bthd', p.astype(vs.dtype), vs)
        return o.reshape(b, Q_BLOCK, N_HEADS * HEAD_DIM)

    out = lax.map(block, jnp.arange(s // Q_BLOCK))
    return out.transpose(1, 0, 2, 3).reshape(b, s, N_HEADS * HEAD_DIM)


def sample_sparse_attention(q, k, v, qi, ki, iw, pool_k, pool_v, pool_idx_k, page_table):
    b, t = q.shape[:2]
    past = page_table.shape[1] * PAGE_SIZE
    topk = min(TOPK_MAX, (past + t) // 4)
    ki_past = pool_idx_k[page_table].reshape(b, past, IDX_DIM).astype(ki.dtype)
    ki_all = jnp.concatenate([ki_past, ki], axis=1)
    q_pos = past + jnp.arange(t)
    key_pos = jnp.arange(past + t)
    sc = indexer_scores(qi, iw, ki_all)
    sc = jnp.where(key_pos[None, None, :] <= q_pos[None, :, None], sc, -jnp.inf)
    _, sel = lax.top_k(sc, topk)
    in_past = sel < past
    logical = jnp.minimum(sel, past - 1)
    phys = jax.vmap(lambda pt, idx: pt[idx])(page_table, logical // PAGE_SIZE)
    off = logical % PAGE_SIZE
    kp = pool_k[phys, off]
    vp = pool_v[phys, off]
    new_pos = past + jnp.arange(t)
    sel_new = jnp.any(sel[..., None] == new_pos, axis=2)
    sel_new = sel_new & (new_pos[None, None, :] <= q_pos[None, :, None])
    lp = jnp.einsum('bthd,btkhd->bthk', q, kp, preferred_element_type=jnp.float32)
    ln = jnp.einsum('bthd,bjhd->bthj', q, k, preferred_element_type=jnp.float32)
    logits = jnp.concatenate([lp, ln], axis=-1) * ATTN_SCALE
    mask = jnp.concatenate([in_past, sel_new], axis=-1)[:, :, None, :]
    p = jax.nn.softmax(jnp.where(mask, logits, -jnp.inf), axis=-1).astype(v.dtype)
    o = (jnp.einsum('bthk,btkhd->bthd', p[..., :topk], vp.astype(v.dtype))
         + jnp.einsum('bthj,bjhd->bthd', p[..., topk:], v))
    return o.reshape(b, t, N_HEADS * HEAD_DIM)


def mixer_sublayer(h, pos, conv_prev, attend, g, w_in, conv_w, q_g, k_g, w_out):
    xn = rms_norm(h, g)
    cb, cc, ch, q, k, v, qi, ki, iw = split_projection(xn @ w_in, pos, q_g, k_g)
    conv_out, conv_state = causal_dwconv3(cc * ch, conv_prev, conv_w)
    att = attend(q, k, v, qi, ki, iw)
    y = jnp.concatenate([cb * conv_out, att.astype(h.dtype)], axis=-1) @ w_out
    return h + y, conv_state, k, v, ki


def memory_kv(mem, g_src, w_kv, k_g):
    b, m, _ = mem.shape
    mk, mv = jnp.split(rms_norm(mem, g_src) @ w_kv, 2, axis=-1)
    mk = rms_norm(mk.reshape(b, m, MEM_HEADS, MEM_HEAD_DIM), k_g)
    return mk, mv.reshape(b, m, MEM_HEADS, MEM_HEAD_DIM)


def memory_sublayer(h, mk, mv, g, w_q, q_g, w_o):
    b, t, _ = h.shape
    q = rms_norm((rms_norm(h, g) @ w_q).reshape(b, t, MEM_HEADS, MEM_HEAD_DIM), q_g)
    logits = jnp.einsum('bthd,bmhd->bhtm', q, mk, preferred_element_type=jnp.float32) * MEM_SCALE
    p = jax.nn.softmax(logits, axis=-1).astype(mv.dtype)
    o = jnp.einsum('bhtm,bmhd->bthd', p, mv).reshape(b, t, MEM_HEADS * MEM_HEAD_DIM)
    return h + (o @ w_o).astype(h.dtype)


def ffn_sublayer(h, prev, g, w_gu, conv_w, conv_b, w_down):
    gate, up = jnp.split(rms_norm(h, g) @ w_gu, 2, axis=-1)
    gate_c, state = causal_dwconv3(gate, prev, conv_w)
    return h + (jax.nn.silu(gate_c + conv_b) * up) @ w_down, state


def setup_inputs(seed: int = 0) -> dict:
    key = jax.random.key(seed)
    ks = jax.random.split(key, 32)

    def nrm(i, shape, scale):
        return jax.random.normal(ks[i], shape, jnp.float32) * scale

    def gain(i, n):
        return 1.0 + 0.1 * jax.random.normal(ks[i], (DEPTH, n), jnp.float32)

    n_pages = PAST_LEN // PAGE_SIZE
    n_used = DEC_BATCH * n_pages
    n_pool = n_used + n_used // 4
    page_table = jax.random.permutation(ks[0], n_pool)[:n_used].reshape(DEC_BATCH, n_pages).astype(jnp.int32)
    return {
        'x_prompt': nrm(1, (BATCH, SEQ, D_MODEL), 1.0),
        'x_sample': nrm(2, (DEC_BATCH, DEC_SEQ, D_MODEL), 1.0),
        'cache_k': nrm(3, (DEPTH, n_pool, PAGE_SIZE, N_HEADS, HEAD_DIM), 1.0),
        'cache_v': nrm(4, (DEPTH, n_pool, PAGE_SIZE, N_HEADS, HEAD_DIM), 1.0),
        'cache_idx_k': nrm(5, (DEPTH, n_pool, PAGE_SIZE, IDX_DIM), 1.0),
        'state_conv_mix': nrm(6, (DEPTH, DEC_BATCH, CONV_W - 1, D_CONV), 1.0),
        'state_conv_ffn': nrm(7, (DEPTH, DEC_BATCH, CONV_W - 1, D_FF), 1.0),
        'cache_mem_k': nrm(8, (DEPTH, DEC_BATCH, N_MEM, MEM_HEADS, MEM_HEAD_DIM), 1.0),
        'cache_mem_v': nrm(9, (DEPTH, DEC_BATCH, N_MEM, MEM_HEADS, MEM_HEAD_DIM), 1.0),
        'page_table': page_table,
        'mem_prompt': nrm(10, (BATCH, N_MEM, D_MODEL), 1.0),
        'g_mix': gain(11, D_MODEL),
        'w_in': nrm(12, (DEPTH, D_MODEL, D_IN), D_MODEL ** -0.5),
        'conv_mix_w': nrm(13, (DEPTH, CONV_W, D_CONV), CONV_W ** -0.5),
        'q_norm_g': gain(14, HEAD_DIM),
        'k_norm_g': gain(15, HEAD_DIM),
        'w_out': nrm(16, (DEPTH, D_MIX, D_MODEL), D_MIX ** -0.5),
        'g_mem': gain(17, D_MODEL),
        'g_mem_src': gain(18, D_MODEL),
        'w_q_mem': nrm(19, (DEPTH, D_MODEL, MEM_HEADS * MEM_HEAD_DIM), D_MODEL ** -0.5),
        'w_kv_mem': nrm(20, (DEPTH, D_MODEL, 2 * MEM_HEADS * MEM_HEAD_DIM), D_MODEL ** -0.5),
        'mq_norm_g': gain(21, MEM_HEAD_DIM),
        'mk_norm_g': gain(22, MEM_HEAD_DIM),
        'w_o_mem': nrm(23, (DEPTH, MEM_HEADS * MEM_HEAD_DIM, D_MODEL), (MEM_HEADS * MEM_HEAD_DIM) ** -0.5),
        'g_ffn': gain(24, D_MODEL),
        'w_gu': nrm(25, (DEPTH, D_MODEL, 2 * D_FF), D_MODEL ** -0.5),
        'conv_ffn_w': nrm(26, (DEPTH, CONV_W, D_FF), CONV_W ** -0.5),
        'conv_ffn_b': nrm(27, (DEPTH, D_FF), 0.02),
        'w_down': nrm(28, (DEPTH, D_FF, D_MODEL), D_FF ** -0.5),
    }


def reference(x_prompt, x_sample, cache_k, cache_v, cache_idx_k, state_conv_mix, state_conv_ffn,
              cache_mem_k, cache_mem_v, page_table, mem_prompt,
              g_mix, w_in, conv_mix_w, q_norm_g, k_norm_g, w_out,
              g_mem, g_mem_src, w_q_mem, w_kv_mem, mq_norm_g, mk_norm_g, w_o_mem,
              g_ffn, w_gu, conv_ffn_w, conv_ffn_b, w_down):
    b, s, _ = x_prompt.shape
    past = page_table.shape[1] * PAGE_SIZE
    pos_p = jnp.arange(s)
    pos_s = past + jnp.arange(x_sample.shape[1])
    zero_mix = jnp.zeros((b, CONV_W - 1, D_CONV), x_prompt.dtype)
    zero_ffn = jnp.zeros((b, CONV_W - 1, D_FF), x_prompt.dtype)
    hp, hs = x_prompt, x_sample
    kp_l, vp_l, ikp_l, cmp_l, cfp_l, mkp_l, mvp_l = [], [], [], [], [], [], []
    ks_l, vs_l, iks_l, cms_l, cfs_l = [], [], [], [], []
    for l in range(DEPTH):
        hp, c_mix, k_new, v_new, ik_new = mixer_sublayer(
            hp, pos_p, zero_mix, prompt_sparse_attention,
            g_mix[l], w_in[l], conv_mix_w[l], q_norm_g[l], k_norm_g[l], w_out[l])
        mk, mv = memory_kv(mem_prompt, g_mem_src[l], w_kv_mem[l], mk_norm_g[l])
        hp = memory_sublayer(hp, mk, mv, g_mem[l], w_q_mem[l], mq_norm_g[l], w_o_mem[l])
        hp, c_ffn = ffn_sublayer(hp, zero_ffn, g_ffn[l], w_gu[l], conv_ffn_w[l], conv_ffn_b[l], w_down[l])
        kp_l.append(k_new); vp_l.append(v_new); ikp_l.append(ik_new)
        cmp_l.append(c_mix); cfp_l.append(c_ffn); mkp_l.append(mk); mvp_l.append(mv)

        attend_s = functools.partial(sample_sparse_attention, pool_k=cache_k[l], pool_v=cache_v[l],
                                     pool_idx_k=cache_idx_k[l], page_table=page_table)
        hs, c_mix_s, k_s, v_s, ik_s = mixer_sublayer(
            hs, pos_s, state_conv_mix[l], attend_s,
            g_mix[l], w_in[l], conv_mix_w[l], q_norm_g[l], k_norm_g[l], w_out[l])
        hs = memory_sublayer(hs, cache_mem_k[l], cache_mem_v[l], g_mem[l], w_q_mem[l], mq_norm_g[l], w_o_mem[l])
        hs, c_ffn_s = ffn_sublayer(hs, state_conv_ffn[l], g_ffn[l], w_gu[l], conv_ffn_w[l], conv_ffn_b[l], w_down[l])
        ks_l.append(k_s); vs_l.append(v_s); iks_l.append(ik_s); cms_l.append(c_mix_s); cfs_l.append(c_ffn_s)

    return (hp, hs,
            jnp.stack(kp_l), jnp.stack(vp_l), jnp.stack(ikp_l), jnp.stack(cmp_l), jnp.stack(cfp_l),
            jnp.stack(mkp_l), jnp.stack(mvp_l),
            jnp.stack(ks_l), jnp.stack(vs_l), jnp.stack(iks_l), jnp.stack(cms_l), jnp.stack(cfs_l))
```

```python
import functools
import math

import jax
import jax.numpy as jnp
from jax import lax
from jax.experimental import pallas as pl
from jax.experimental.pallas import tpu as pltpu

D_MODEL = 1024
D_CONV = 512
N_HEADS = 8
HEAD_DIM = 64
N_IDX_HEADS = 8
IDX_DIM = 64
TOPK_MAX = 256
ROPE_THETA = 10000.0
MEM_HEADS = 4
MEM_HEAD_DIM = 256
D_FF = 2816
PAGE_SIZE = 128
EPS = 1e-6
ATTN_SCALE = HEAD_DIM ** -0.5
MEM_SCALE = MEM_HEAD_DIM ** -0.5
IDX_SCALE = (N_IDX_HEADS * IDX_DIM) ** -0.5
D_ATT = N_HEADS * HEAD_DIM
D_IDX = N_IDX_HEADS * IDX_DIM
N_WIDE = 7

LANES = 128
SUBLANES = 8
VMEM_LIMIT = 56 * 1024 * 1024

TILE = 256
INT_MIN = -(2 ** 31)


def _rms(x, g):
    ms = jnp.mean(x * x, axis=-1, keepdims=True)
    return (x * lax.rsqrt(ms + EPS)) * g


def _lane_iota(shape):
    return lax.broadcasted_iota(jnp.int32, shape, len(shape) - 1)


def _row_iota(shape):
    return lax.broadcasted_iota(jnp.int32, shape, len(shape) - 2)


def _head_rms(x, g2):
    low = _lane_iota(x.shape) < HEAD_DIM
    ss = x * x
    s_lo = jnp.sum(jnp.where(low, ss, 0.0), axis=-1, keepdims=True)
    s_hi = jnp.sum(jnp.where(low, 0.0, ss), axis=-1, keepdims=True)
    r = jnp.where(low, lax.rsqrt(s_lo * (1.0 / HEAD_DIM) + EPS), lax.rsqrt(s_hi * (1.0 / HEAD_DIM) + EPS))
    return (x * r) * g2


def _rope(x, cos, sin_signed):
    w = x.shape[-1]
    half = HEAD_DIM // 2
    fwd = pltpu.roll(x, half, axis=1)
    bwd = pltpu.roll(x, w - half, axis=1)
    first = (_lane_iota(x.shape) % HEAD_DIM) < half
    return x * cos + jnp.where(first, bwd, fwd) * sin_signed


def _shifted_rows(u, carry, prevpad, seq8):
    t = u.shape[0]
    r1 = pltpu.roll(u, 1, axis=0)
    r2 = pltpu.roll(u, 2, axis=0)
    row = _row_iota(u.shape)
    if seq8:
        in_seq = row % SUBLANES
        r1 = jnp.where(in_seq == 0, pltpu.roll(prevpad, t - 1, axis=0), r1)
        r2 = jnp.where(in_seq < 2, prevpad, r2)
    else:
        c1 = jnp.broadcast_to(carry[SUBLANES - 1:SUBLANES, :], u.shape)
        c2 = jnp.broadcast_to(carry[SUBLANES - 2:SUBLANES - 1, :], u.shape)
        r1 = jnp.where(row == 0, c1, r1)
        r2 = jnp.where(row == 0, c2, jnp.where(row == 1, c1, r2))
    return r1, r2


def _proj_kernel(*refs, seq8):
    if seq8:
        (x_ref, g_ref, wa_ref, wb_ref, cw_ref, qg_ref, kg_ref, cos_ref, sin_ref, prev_ref,
         u_ref, tail_ref, kf_ref, vf_ref, kib_ref, qf_ref, qif_ref) = refs
        carry_ref = None
    else:
        (x_ref, g_ref, wa_ref, wb_ref, cw_ref, qg_ref, kg_ref, cos_ref, sin_ref,
         u_ref, tail_ref, kf_ref, vf_ref, kib_ref, qh_ref, qih_ref, kt_ref, kit_ref, vb_ref, carry_ref) = refs

    x = x_ref[0]
    xn = _rms(x, g_ref[...]).astype(jnp.bfloat16)

    def piece(p):
        return jnp.dot(xn, wa_ref[:, p * D_CONV:(p + 1) * D_CONV], preferred_element_type=jnp.float32)

    cb, cc, ch = piece(0), piece(1), piece(2)
    u = cc * ch
    if seq8:
        r1, r2 = _shifted_rows(u, None, prev_ref[0], True)
        tail_ref[0] = u
    else:
        @pl.when(pl.program_id(1) == 0)
        def _():
            carry_ref[...] = jnp.zeros_like(carry_ref)
        r1, r2 = _shifted_rows(u, carry_ref[...], None, False)
        tail = u[u.shape[0] - SUBLANES:, :]
        carry_ref[...] = tail
        tail_ref[0] = tail
    cw = cw_ref[...]
    conv = r2 * cw[0:1, :] + r1 * cw[1:2, :] + u * cw[2:3, :]
    u_ref[0] = (cb * conv).astype(u_ref.dtype)

    cos = cos_ref[...]
    sin = sin_ref[...]
    cos4 = jnp.concatenate([cos] * (D_ATT // LANES), axis=1)
    sin4 = jnp.concatenate([sin] * (D_ATT // LANES), axis=1)

    def normed(z, g_ref_):
        g2 = g_ref_[...]
        tiles = [_head_rms(z[:, c * LANES:(c + 1) * LANES], g2) for c in range(D_ATT // LANES)]
        return jnp.concatenate(tiles, axis=1)

    q = _rope(normed(piece(3), qg_ref), cos4, sin4) * ATTN_SCALE
    k = _rope(normed(piece(4), kg_ref), cos4, sin4)
    v = piece(5)
    qi = _rope(piece(6), cos4, sin4)
    zb = jnp.dot(xn, wb_ref[...], preferred_element_type=jnp.float32)
    lane = _lane_iota(zb.shape)
    kib = jnp.where(lane < IDX_DIM, _rope(zb, cos, sin), zb * IDX_SCALE)
    kf_ref[0] = k
    vf_ref[0] = v
    kib_ref[0] = kib
    if seq8:
        qf_ref[0] = q
        qif_ref[0] = qi
    else:
        for h in range(N_HEADS):
            qh_ref[0, h] = q[:, h * HEAD_DIM:(h + 1) * HEAD_DIM].astype(jnp.bfloat16)
            qih_ref[0, h] = qi[:, h * IDX_DIM:(h + 1) * IDX_DIM].astype(jnp.bfloat16)
        kt = k.T
        for h in range(N_HEADS):
            kt_ref[0, 0, h] = kt[h * HEAD_DIM:(h + 1) * HEAD_DIM, :].astype(jnp.bfloat16)
        kit_ref[0, 0] = kib.T[:IDX_DIM, :].astype(jnp.bfloat16)
        vb_ref[0] = v.astype(jnp.bfloat16)


def _rope_tables(pos):
    half = HEAD_DIM // 2
    inv = jnp.exp(jnp.arange(half, dtype=jnp.float32) * (-2.0 * math.log(ROPE_THETA) / HEAD_DIM))
    ang = pos.astype(jnp.float32)[:, None] * inv[None, :]
    cos = jnp.cos(ang)
    sin = jnp.sin(ang)
    cos_t = jnp.concatenate([cos, cos] * (LANES // HEAD_DIM), axis=1)
    sin_t = jnp.concatenate([-sin, sin] * (LANES // HEAD_DIM), axis=1)
    return cos_t, sin_t


def _const_spec(shape):
    nd = len(shape)
    return pl.BlockSpec(shape, lambda *_: (0,) * nd)


def _proj(x, prevpad, pos, g, wa, wb, cw, qg, kg, *, seq8):
    b, s, d = x.shape
    tm = TILE
    nt = s // tm
    cos_t, sin_t = _rope_tables(pos)
    row3 = lambda bi, i: (bi, i, 0)
    in_specs = [
        pl.BlockSpec((1, tm, d), row3),
        _const_spec((1, d)),
        _const_spec(wa.shape),
        _const_spec(wb.shape),
        _const_spec(cw.shape),
        _const_spec((1, LANES)),
        _const_spec((1, LANES)),
        pl.BlockSpec((tm, LANES), lambda bi, i: (i, 0)),
        pl.BlockSpec((tm, LANES), lambda bi, i: (i, 0)),
    ]
    args = [x, g, wa, wb, cw, qg, kg, cos_t, sin_t]
    f32 = jnp.float32
    out_shape = [
        jax.ShapeDtypeStruct((b, s, D_CONV), jnp.bfloat16),
    ]
    out_specs = [pl.BlockSpec((1, tm, D_CONV), row3)]
    if seq8:
        in_specs.append(pl.BlockSpec((1, tm, D_CONV), row3))
        args.append(prevpad)
        out_shape.append(jax.ShapeDtypeStruct((b, s, D_CONV), f32))
        out_specs.append(pl.BlockSpec((1, tm, D_CONV), row3))
    else:
        out_shape.append(jax.ShapeDtypeStruct((b, SUBLANES, D_CONV), f32))
        out_specs.append(pl.BlockSpec((1, SUBLANES, D_CONV), lambda bi, i: (bi, 0, 0)))
    out_shape += [
        jax.ShapeDtypeStruct((b, s, D_ATT), f32),
        jax.ShapeDtypeStruct((b, s, D_ATT), f32),
        jax.ShapeDtypeStruct((b, s, LANES), f32),
    ]
    out_specs += [pl.BlockSpec((1, tm, D_ATT), row3), pl.BlockSpec((1, tm, D_ATT), row3),
                  pl.BlockSpec((1, tm, LANES), row3)]
    scratch = []
    if seq8:
        out_shape += [jax.ShapeDtypeStruct((b, s, D_ATT), f32), jax.ShapeDtypeStruct((b, s, D_IDX), f32)]
        out_specs += [pl.BlockSpec((1, tm, D_ATT), row3), pl.BlockSpec((1, tm, D_IDX), row3)]
    else:
        bf = jnp.bfloat16
        out_shape += [
            jax.ShapeDtypeStruct((b, N_HEADS, s, HEAD_DIM), bf),
            jax.ShapeDtypeStruct((b, N_IDX_HEADS, s, IDX_DIM), bf),
            jax.ShapeDtypeStruct((b, nt, N_HEADS, HEAD_DIM, tm), bf),
            jax.ShapeDtypeStruct((b, nt, IDX_DIM, tm), bf),
            jax.ShapeDtypeStruct((b, s, D_ATT), bf),
        ]
        out_specs += [
            pl.BlockSpec((1, N_HEADS, tm, HEAD_DIM), lambda bi, i: (bi, 0, i, 0)),
            pl.BlockSpec((1, N_IDX_HEADS, tm, IDX_DIM), lambda bi, i: (bi, 0, i, 0)),
            pl.BlockSpec((1, 1, N_HEADS, HEAD_DIM, tm), lambda bi, i: (bi, i, 0, 0, 0)),
            pl.BlockSpec((1, 1, IDX_DIM, tm), lambda bi, i: (bi, i, 0, 0)),
            pl.BlockSpec((1, tm, D_ATT), row3),
        ]
        scratch = [pltpu.VMEM((SUBLANES, D_CONV), f32)]
    outs = pl.pallas_call(
        functools.partial(_proj_kernel, seq8=seq8),
        grid=(b, nt),
        in_specs=in_specs,
        out_specs=out_specs,
        out_shape=out_shape,
        scratch_shapes=scratch,
        compiler_params=pltpu.CompilerParams(
            dimension_semantics=("arbitrary", "arbitrary"), vmem_limit_bytes=VMEM_LIMIT),
        name="proj_sample" if seq8 else "proj_prompt",
    )(*args)
    return outs


NEG = -1e30
BIS_ROWS = 128


def _sortable(x):
    b = pltpu.bitcast(x, jnp.int32)
    return b ^ ((b >> 31) & 0x7FFFFFFF)


def _rep(x, n):
    return x if n == 1 else jnp.concatenate([x] * n, axis=1)


def _kth_largest_key(load_block, n_blocks, block_w, rows, k):
    def step(it, prefix):
        cand = prefix + jnp.left_shift(jnp.int32(1), 31 - it)

        def count(j, cnt):
            blk = load_block(j)
            for c in range(block_w // LANES):
                cnt = cnt + jnp.where(blk[:, c * LANES:(c + 1) * LANES] >= cand, 1, 0)
            return cnt

        cnt = lax.fori_loop(0, n_blocks, count, jnp.zeros((rows, LANES), jnp.int32))
        total = jnp.sum(cnt, axis=-1, keepdims=True)
        return jnp.where(total >= k, cand, prefix)

    return lax.fori_loop(0, 32, step, jnp.full((rows, LANES), INT_MIN, jnp.int32))


def _count_ge_gt(load_block, n_blocks, block_w, rows, v):
    def count(j, carry):
        ge, gt = carry
        blk = load_block(j)
        for c in range(block_w // LANES):
            t = blk[:, c * LANES:(c + 1) * LANES]
            ge = ge + jnp.where(t >= v, 1, 0)
            gt = gt + jnp.where(t > v, 1, 0)
        return ge, gt

    z = jnp.zeros((rows, LANES), jnp.int32)
    ge, gt = lax.fori_loop(0, n_blocks, count, (z, z))
    return jnp.sum(ge, axis=-1, keepdims=True), jnp.sum(gt, axis=-1, keepdims=True)


def _dsa_prompt_kernel(qh_ref, qih_ref, kib_ref, kt_ref, kit_ref, vb_ref, o_ref,
                       sc_ref, wb_ref, vs_ref, need_ref, tie_ref, m_ref, l_ref, acc_ref, *, topk):
    i = pl.program_id(1)
    tq = tk = TILE
    nl = tk // LANES
    f32 = jnp.float32

    for h in range(N_IDX_HEADS):
        wb_ref[h] = jnp.broadcast_to(kib_ref[0, :, IDX_DIM + h:IDX_DIM + h + 1], (tq, LANES))

    def score_block(j, diagonal):
        kit = kit_ref[0, j]
        acc = jnp.zeros((tq, tk), f32)
        for h in range(N_IDX_HEADS):
            d = jnp.dot(qih_ref[0, h], kit, preferred_element_type=f32)
            acc = acc + jnp.maximum(d, 0.0) * _rep(wb_ref[h], nl)
        key = _sortable(acc)
        if diagonal:
            key = jnp.where(_lane_iota(key.shape) > _row_iota(key.shape), INT_MIN, key)
        sc_ref[j] = key

    def score_body(j, c):
        score_block(j, False)
        return c

    lax.fori_loop(0, i, score_body, 0)
    score_block(i, True)

    flag = jnp.int32(0)
    for rb in range(tq // BIS_ROWS):
        rows = slice(rb * BIS_ROWS, (rb + 1) * BIS_ROWS)
        load = lambda j: sc_ref[j, rows, :]
        vstar = _kth_largest_key(load, i + 1, tk, BIS_ROWS, topk)
        vs = jnp.maximum(vstar, INT_MIN + 1)
        n_ge, n_gt = _count_ge_gt(load, i + 1, tk, BIS_ROWS, vs)
        vs_ref[rows, :] = vs
        need_ref[rows, :] = jnp.broadcast_to((topk - n_gt).astype(f32), (BIS_ROWS, LANES))
        flag = jnp.maximum(flag, jnp.max(jnp.where(n_ge > topk, 1, 0)))

    m_ref[...] = jnp.full(m_ref.shape, NEG, f32)
    l_ref[...] = jnp.zeros(l_ref.shape, f32)
    acc_ref[...] = jnp.zeros(acc_ref.shape, f32)
    tie_ref[...] = jnp.zeros(tie_ref.shape, f32)

    def attend(j, ties):
        key = sc_ref[j]
        vs = _rep(vs_ref[...], nl)
        if ties:
            eq = key == vs
            upper = (_row_iota((tk, tk)) <= _lane_iota((tk, tk))).astype(jnp.bfloat16)
            e = jnp.where(eq, 1.0, 0.0)
            rank = jnp.dot(e.astype(jnp.bfloat16), upper, preferred_element_type=f32) + _rep(tie_ref[...], nl)
            tie_ref[...] = tie_ref[...] + jnp.sum(e, axis=-1, keepdims=True)
            mask = (key > vs) | (eq & (rank <= _rep(need_ref[...], nl)))
        else:
            mask = key >= vs
        for h in range(N_HEADS):
            s = jnp.dot(qh_ref[0, h], kt_ref[0, j, h], preferred_element_type=f32)
            sm = jnp.where(mask, s, NEG)
            m_old = m_ref[h]
            m_new = jnp.maximum(m_old, jnp.max(sm, axis=-1, keepdims=True))
            alpha = jnp.exp(m_old - m_new)
            p = jnp.exp(sm - _rep(m_new, nl))
            l_ref[h] = alpha * l_ref[h] + jnp.sum(p, axis=-1, keepdims=True)
            c = h // 2
            pv = jnp.dot(p.astype(jnp.bfloat16), vb_ref[0, j, :, c * LANES:(c + 1) * LANES],
                         preferred_element_type=f32)
            acc_ref[h] = alpha * acc_ref[h] + pv
            m_ref[h] = m_new

    def run(ties):
        def body(j, c):
            attend(j, ties)
            return c
        lax.fori_loop(0, i + 1, body, 0)

    @pl.when(flag == 0)
    def _():
        run(False)

    @pl.when(flag != 0)
    def _():
        run(True)

    low = _lane_iota((tq, LANES)) < HEAD_DIM
    for c in range(N_HEADS // 2):
        even = acc_ref[2 * c] / l_ref[2 * c]
        odd = acc_ref[2 * c + 1] / l_ref[2 * c + 1]
        o_ref[0, :, c * LANES:(c + 1) * LANES] = jnp.where(low, even, odd).astype(o_ref.dtype)


def _dsa_prompt(qh, qih, kib, kt, kit, vb):
    b, _, s, _ = qh.shape
    t = TILE
    nt = s // t
    topk = min(TOPK_MAX, s // 4)
    vb4 = vb.reshape(b, nt, t, D_ATT)
    resident = dict(pipeline_mode=pl.Buffered(1))
    return pl.pallas_call(
        functools.partial(_dsa_prompt_kernel, topk=topk),
        grid=(b, nt),
        in_specs=[
            pl.BlockSpec((1, N_HEADS, t, HEAD_DIM), lambda bi, i: (bi, 0, i, 0)),
            pl.BlockSpec((1, N_IDX_HEADS, t, IDX_DIM), lambda bi, i: (bi, 0, i, 0)),
            pl.BlockSpec((1, t, LANES), lambda bi, i: (bi, i, 0)),
            pl.BlockSpec((1, nt, N_HEADS, HEAD_DIM, t), lambda bi, i: (bi, 0, 0, 0, 0), **resident),
            pl.BlockSpec((1, nt, IDX_DIM, t), lambda bi, i: (bi, 0, 0, 0), **resident),
            pl.BlockSpec((1, nt, t, D_ATT), lambda bi, i: (bi, 0, 0, 0), **resident),
        ],
        out_specs=pl.BlockSpec((1, t, D_ATT), lambda bi, i: (bi, i, 0)),
        out_shape=jax.ShapeDtypeStruct((b, s, D_ATT), jnp.bfloat16),
        scratch_shapes=[
            pltpu.VMEM((nt, t, t), jnp.int32),
            pltpu.VMEM((N_IDX_HEADS, t, LANES), jnp.float32),
            pltpu.VMEM((t, LANES), jnp.int32),
            pltpu.VMEM((t, LANES), jnp.float32),
            pltpu.VMEM((t, LANES), jnp.float32),
            pltpu.VMEM((N_HEADS, t, LANES), jnp.float32),
            pltpu.VMEM((N_HEADS, t, LANES), jnp.float32),
            pltpu.VMEM((N_HEADS, t, LANES), jnp.float32),
        ],
        compiler_params=pltpu.CompilerParams(
            dimension_semantics=("arbitrary", "arbitrary"), vmem_limit_bytes=VMEM_LIMIT),
        name="dsa_prompt",
    )(qh, qih, kib, kt, kit, vb4)


def _mem_head_rms(x, g_ref):
    g = g_ref[...]
    heads = [_rms(x[:, h * MEM_HEAD_DIM:(h + 1) * MEM_HEAD_DIM], g) for h in range(MEM_HEADS)]
    return jnp.concatenate(heads, axis=1)


def _memkv_kernel(mem_ref, g_ref, w_ref, kg_ref, mk_ref, mv_ref):
    xn = _rms(mem_ref[0], g_ref[...]).astype(jnp.bfloat16)
    d = MEM_HEADS * MEM_HEAD_DIM
    mk = jnp.dot(xn, w_ref[:, :d], preferred_element_type=jnp.float32)
    mk_ref[0] = _mem_head_rms(mk, kg_ref)
    mv_ref[0] = jnp.dot(xn, w_ref[:, d:], preferred_element_type=jnp.float32)


def _memkv(mem, g, w_kv, kg):
    b, m, d = mem.shape
    dk = MEM_HEADS * MEM_HEAD_DIM
    blk = pl.BlockSpec((1, m, dk), lambda bi: (bi, 0, 0))
    return pl.pallas_call(
        _memkv_kernel,
        grid=(b,),
        in_specs=[pl.BlockSpec((1, m, d), lambda bi: (bi, 0, 0)), _const_spec((1, d)),
                  _const_spec(w_kv.shape), _const_spec((1, MEM_HEAD_DIM))],
        out_specs=[blk, blk],
        out_shape=[jax.ShapeDtypeStruct((b, m, dk), jnp.float32)] * 2,
        compiler_params=pltpu.CompilerParams(dimension_semantics=("arbitrary",), vmem_limit_bytes=VMEM_LIMIT),
        name="memory_kv",
    )(mem, g, w_kv, kg)


def _post_attn_kernel(x_ref, u_ref, att_ref, wo_ref, g_ref, wq_ref, qg_ref, h_ref, qm_ref):
    y = jnp.dot(u_ref[0].astype(jnp.bfloat16), wo_ref[:D_CONV, :], preferred_element_type=jnp.float32)
    y = y + jnp.dot(att_ref[0].astype(jnp.bfloat16), wo_ref[D_CONV:, :], preferred_element_type=jnp.float32)
    h = x_ref[0] + y
    h_ref[0] = h
    q = jnp.dot(_rms(h, g_ref[...]).astype(jnp.bfloat16), wq_ref[...], preferred_element_type=jnp.float32)
    qm_ref[0] = (_mem_head_rms(q, qg_ref) * MEM_SCALE).astype(qm_ref.dtype)


def _post_attn(x, u, att, wo, g, wq, qg, qm_dtype):
    b, s, d = x.shape
    tm = TILE
    row3 = lambda bi, i: (bi, i, 0)
    return pl.pallas_call(
        _post_attn_kernel,
        grid=(b, s // tm),
        in_specs=[pl.BlockSpec((1, tm, d), row3), pl.BlockSpec((1, tm, D_CONV), row3),
                  pl.BlockSpec((1, tm, D_ATT), row3), _const_spec(wo.shape), _const_spec((1, d)),
                  _const_spec(wq.shape), _const_spec((1, MEM_HEAD_DIM))],
        out_specs=[pl.BlockSpec((1, tm, d), row3), pl.BlockSpec((1, tm, d), row3)],
        out_shape=[jax.ShapeDtypeStruct((b, s, d), jnp.float32), jax.ShapeDtypeStruct((b, s, d), qm_dtype)],
        compiler_params=pltpu.CompilerParams(
            dimension_semantics=("arbitrary", "arbitrary"), vmem_limit_bytes=VMEM_LIMIT),
        name="post_attn",
    )(x, u, att, wo, g, wq, qg)


def _mem_attn_kernel(q_ref, mk_ref, mv_ref, o_ref):
    q = q_ref[0].astype(jnp.bfloat16)
    nt_dims = (((1,), (1,)), ((), ()))
    for h in range(MEM_HEADS):
        cols = slice(h * MEM_HEAD_DIM, (h + 1) * MEM_HEAD_DIM)
        logits = lax.dot_general(q[:, cols], mk_ref[0, :, cols].astype(jnp.bfloat16), nt_dims,
                                 preferred_element_type=jnp.float32)
        m = jnp.max(logits, axis=-1, keepdims=True)
        p = jnp.exp(logits - m)
        l = jnp.sum(p, axis=-1, keepdims=True)
        o = jnp.dot(p.astype(jnp.bfloat16), mv_ref[0, :, cols].astype(jnp.bfloat16),
                    preferred_element_type=jnp.float32)
        o_ref[0, :, cols] = (o / l).astype(o_ref.dtype)


def _mem_attn(qm, mk, mv, tm):
    g, s, d = qm.shape
    m = mk.shape[1]
    row3 = lambda gi, i: (gi, i, 0)
    grp = lambda gi, i: (gi, 0, 0)
    return pl.pallas_call(
        _mem_attn_kernel,
        grid=(g, s // tm),
        in_specs=[pl.BlockSpec((1, tm, d), row3), pl.BlockSpec((1, m, d), grp), pl.BlockSpec((1, m, d), grp)],
        out_specs=pl.BlockSpec((1, tm, d), row3),
        out_shape=jax.ShapeDtypeStruct((g, s, d), qm.dtype),
        compiler_params=pltpu.CompilerParams(
            dimension_semantics=("arbitrary", "arbitrary"), vmem_limit_bytes=VMEM_LIMIT),
        name="mem_attn",
    )(qm, mk, mv)


def _ffn_kernel(*refs, seq8):
    if seq8:
        h_ref, o_ref, wo_ref, g_ref, wg_ref, wu_ref, cw_ref, cb_ref, wd_ref, prev_ref, y_ref, tail_ref = refs
        carry_ref = None
    else:
        h_ref, o_ref, wo_ref, g_ref, wg_ref, wu_ref, cw_ref, cb_ref, wd_ref, y_ref, tail_ref, carry_ref = refs
    h = h_ref[0] + jnp.dot(o_ref[0].astype(jnp.bfloat16), wo_ref[...], preferred_element_type=jnp.float32)
    xn = _rms(h, g_ref[...]).astype(jnp.bfloat16)
    gate = jnp.dot(xn, wg_ref[...], preferred_element_type=jnp.float32)
    up = jnp.dot(xn, wu_ref[...], preferred_element_type=jnp.float32)
    if seq8:
        r1, r2 = _shifted_rows(gate, None, prev_ref[0], True)
        tail_ref[0] = gate
    else:
        @pl.when(pl.program_id(1) == 0)
        def _():
            carry_ref[...] = jnp.zeros_like(carry_ref)
        r1, r2 = _shifted_rows(gate, carry_ref[...], None, False)
        tail = gate[gate.shape[0] - SUBLANES:, :]
        carry_ref[...] = tail
        tail_ref[0] = tail
    cw = cw_ref[...]
    gc = r2 * cw[0:1, :] + r1 * cw[1:2, :] + gate * cw[2:3, :] + cb_ref[...]
    act = (gc / (1.0 + jnp.exp(-gc))) * up
    y_ref[0] = h + jnp.dot(act.astype(jnp.bfloat16), wd_ref[...], preferred_element_type=jnp.float32)


def _ffn(h, o, prevpad, wo, g, wg, wu, cw, cb, wd, *, seq8):
    b, s, d = h.shape
    tm = TILE
    row3 = lambda bi, i: (bi, i, 0)
    resident = dict(pipeline_mode=pl.Buffered(1))
    wspec = lambda w: pl.BlockSpec(w.shape, lambda bi, i: (0, 0), **resident)
    in_specs = [pl.BlockSpec((1, tm, d), row3), pl.BlockSpec((1, tm, d), row3), wspec(wo), _const_spec((1, d)),
                wspec(wg), wspec(wu), _const_spec(cw.shape), _const_spec((1, D_FF)), wspec(wd)]
    args = [h, o, wo, g, wg, wu, cw, cb, wd]
    out_shape = [jax.ShapeDtypeStruct((b, s, d), jnp.float32)]
    out_specs = [pl.BlockSpec((1, tm, d), row3)]
    scratch = []
    if seq8:
        in_specs.append(pl.BlockSpec((1, tm, D_FF), row3))
        args.append(prevpad)
        out_shape.append(jax.ShapeDtypeStruct((b, s, D_FF), jnp.float32))
        out_specs.append(pl.BlockSpec((1, tm, D_FF), row3))
    else:
        out_shape.append(jax.ShapeDtypeStruct((b, SUBLANES, D_FF), jnp.float32))
        out_specs.append(pl.BlockSpec((1, SUBLANES, D_FF), lambda bi, i: (bi, 0, 0)))
        scratch = [pltpu.VMEM((SUBLANES, D_FF), jnp.float32)]
    return pl.pallas_call(
        functools.partial(_ffn_kernel, seq8=seq8),
        grid=(b, s // tm),
        in_specs=in_specs,
        out_specs=out_specs,
        out_shape=out_shape,
        scratch_shapes=scratch,
        compiler_params=pltpu.CompilerParams(
            dimension_semantics=("arbitrary", "arbitrary"), vmem_limit_bytes=VMEM_LIMIT),
        name="ffn_sample" if seq8 else "ffn_prompt",
    )(*args)


def _page_copies(pt_ref, b, n_pages, pool_ref, buf_ref, sem_ref, slot):
    return [pltpu.make_async_copy(pool_ref.at[pt_ref[b, p]], buf_ref.at[slot, p], sem_ref.at[slot])
            for p in range(n_pages)]


def _paged_fetch(pt_ref, n_pages, pools, bufs, sems):
    b = pl.program_id(0)
    nb = pl.num_programs(0)
    slot = b % 2

    def start(seq, s):
        for pool, buf, sem in zip(pools, bufs, sems):
            for cp in _page_copies(pt_ref, seq, n_pages, pool, buf, sem, s):
                cp.start()

    @pl.when(b == 0)
    def _():
        start(b, slot)

    @pl.when(b + 1 < nb)
    def _():
        start(b + 1, 1 - slot)

    for pool, buf, sem in zip(pools, bufs, sems):
        for cp in _page_copies(pt_ref, b, n_pages, pool, buf, sem, slot):
            cp.wait()
    return slot


def _stack_heads(x, width):
    return jnp.concatenate([x[:, h * width:(h + 1) * width] for h in range(x.shape[1] // width)], axis=0)


def _pad_rows(x, rows):
    return jnp.concatenate([x, jnp.zeros((rows - x.shape[0], x.shape[1]), x.dtype)], axis=0)


NT_DIMS = (((1,), (1,)), ((), ()))


def _idx_sample_kernel(pt_ref, qi_ref, kib_ref, pool_ref, sc_ref, buf_ref, sem_ref, *, n_pages):
    slot = _paged_fetch(pt_ref, n_pages, [pool_ref], [buf_ref], [sem_ref])
    f32 = jnp.float32
    ts = qi_ref.shape[1]
    kib = kib_ref[0]
    qis = _stack_heads(qi_ref[0], IDX_DIM).astype(jnp.bfloat16)
    w = [jnp.broadcast_to(kib[:, IDX_DIM + h:IDX_DIM + h + 1], (ts, LANES)) for h in range(N_IDX_HEADS)]

    def scores(keys):
        d = lax.dot_general(qis, keys.astype(jnp.bfloat16), NT_DIMS, preferred_element_type=f32)
        acc = jnp.zeros((ts, LANES), f32)
        for h in range(N_IDX_HEADS):
            acc = acc + jnp.maximum(d[h * ts:(h + 1) * ts, :], 0.0) * w[h]
        return _sortable(acc)

    for p in range(n_pages):
        sc_ref[:, p * PAGE_SIZE:(p + 1) * PAGE_SIZE] = scores(buf_ref[slot, p])
    new = scores(_pad_rows(kib[:, :IDX_DIM], LANES))
    causal = _lane_iota((ts, LANES)) <= _row_iota((ts, LANES))
    sc_ref[:, n_pages * PAGE_SIZE:] = jnp.where(causal, new, INT_MIN)


def _idx_sample(page_table, qif, kib, pool_idx):
    nb, n_pages = page_table.shape
    ts = qif.shape[0] // nb
    width = n_pages * PAGE_SIZE + LANES
    grid_spec = pltpu.PrefetchScalarGridSpec(
        num_scalar_prefetch=1,
        grid=(nb,),
        in_specs=[pl.BlockSpec((1, ts, D_IDX), lambda b, pt: (b, 0, 0)),
                  pl.BlockSpec((1, ts, LANES), lambda b, pt: (b, 0, 0)),
                  pl.BlockSpec(memory_space=pl.ANY)],
        out_specs=pl.BlockSpec((ts, width), lambda b, pt: (b, 0)),
        scratch_shapes=[pltpu.VMEM((2, n_pages, PAGE_SIZE, IDX_DIM), jnp.float32),
                        pltpu.SemaphoreType.DMA((2,))],
    )
    return pl.pallas_call(
        functools.partial(_idx_sample_kernel, n_pages=n_pages),
        grid_spec=grid_spec,
        out_shape=jax.ShapeDtypeStruct((nb * ts, width), jnp.int32),
        compiler_params=pltpu.CompilerParams(dimension_semantics=("arbitrary",), vmem_limit_bytes=VMEM_LIMIT),
        name="idx_sample",
    )(page_table, qif.reshape(nb, ts, D_IDX), kib.reshape(nb, ts, LANES), pool_idx)


def _threshold_kernel(sc_ref, vs_ref, need_ref, flag_ref, *, topk):
    rows, width = sc_ref.shape
    load = lambda j: sc_ref[...]
    vs = jnp.maximum(_kth_largest_key(load, 1, width, rows, topk), INT_MIN + 1)
    n_ge, n_gt = _count_ge_gt(load, 1, width, rows, vs)
    vs_ref[...] = vs
    need_ref[...] = jnp.broadcast_to((topk - n_gt).astype(jnp.float32), (rows, LANES))
    flag_ref[...] = jnp.broadcast_to(jnp.where(n_ge > topk, 1, 0), (rows, LANES))


def _threshold(sc, topk):
    n, width = sc.shape
    rows = BIS_ROWS
    blk = pl.BlockSpec((rows, LANES), lambda i: (i, 0))
    return pl.pallas_call(
        functools.partial(_threshold_kernel, topk=topk),
        grid=(n // rows,),
        in_specs=[pl.BlockSpec((rows, width), lambda i: (i, 0))],
        out_specs=[blk, blk, blk],
        out_shape=[jax.ShapeDtypeStruct((n, LANES), jnp.int32), jax.ShapeDtypeStruct((n, LANES), jnp.float32),
                   jax.ShapeDtypeStruct((n, LANES), jnp.int32)],
        compiler_params=pltpu.CompilerParams(dimension_semantics=("arbitrary",), vmem_limit_bytes=VMEM_LIMIT),
        name="threshold_sample",
    )(sc)


def _dsa_sample_kernel(pt_ref, q_ref, kn_ref, vn_ref, sc_ref, vs_ref, need_ref, flag_ref, kpool_ref, vpool_ref,
                       o_ref, kbuf_ref, vbuf_ref, ksem_ref, vsem_ref, lg_ref, *, n_pages):
    slot = _paged_fetch(pt_ref, n_pages, [kpool_ref, vpool_ref], [kbuf_ref, vbuf_ref], [ksem_ref, vsem_ref])
    f32 = jnp.float32
    bf16 = jnp.bfloat16
    ts = q_ref.shape[1]
    nblk = n_pages + 1
    q = q_ref[0]
    lane = _lane_iota(q.shape)
    head_of_lane = lane // HEAD_DIM
    qx = jnp.concatenate([jnp.where(head_of_lane == h, q, 0.0) for h in range(N_HEADS)], axis=0).astype(bf16)

    key = sc_ref[...]
    vs = vs_ref[...]
    has_ties = jnp.max(flag_ref[...]) > 0

    def tile_h(x):
        return jnp.concatenate([x] * N_HEADS, axis=0)

    def plain_mask(c):
        return jnp.where(key[:, c * LANES:(c + 1) * LANES] >= vs, 1, 0)

    def store_logits(masks):
        for c in range(nblk):
            keys = kbuf_ref[slot, c] if c < n_pages else _pad_rows(kn_ref[0], LANES)
            s = lax.dot_general(qx, keys.astype(bf16), NT_DIMS, preferred_element_type=f32)
            lg_ref[:, c * LANES:(c + 1) * LANES] = jnp.where(tile_h(masks(c)) > 0, s, NEG)

    @pl.when(jnp.logical_not(has_ties))
    def _():
        store_logits(plain_mask)

    @pl.when(has_ties)
    def _():
        upper = (_row_iota((LANES, LANES)) <= _lane_iota((LANES, LANES))).astype(bf16)
        need = need_ref[...]
        seen = jnp.zeros((ts, LANES), f32)
        masks = []
        for c in range(nblk):
            kc = key[:, c * LANES:(c + 1) * LANES]
            eq = kc == vs
            e = jnp.where(eq, 1.0, 0.0)
            rank = jnp.dot(e.astype(bf16), upper, preferred_element_type=f32) + seen
            seen = seen + jnp.sum(e, axis=-1, keepdims=True)
            masks.append(jnp.where((kc > vs) | (eq & (rank <= need)), 1, 0))
        store_logits(lambda c: masks[c])

    lg = lg_ref[...]
    m = jnp.max(lg, axis=-1, keepdims=True)
    p = jnp.exp(lg - m)
    l = jnp.sum(p, axis=-1, keepdims=True)
    pb = p.astype(bf16)
    acc = jnp.zeros((N_HEADS * ts, D_ATT), f32)
    for c in range(nblk):
        vals = vbuf_ref[slot, c] if c < n_pages else _pad_rows(vn_ref[0], LANES)
        acc = acc + jnp.dot(pb[:, c * LANES:(c + 1) * LANES], vals.astype(bf16), preferred_element_type=f32)
    acc = acc / l
    out = jnp.zeros((ts, D_ATT), f32)
    for h in range(N_HEADS):
        out = out + jnp.where(head_of_lane == h, acc[h * ts:(h + 1) * ts, :], 0.0)
    o_ref[0] = out


def _dsa_sample(page_table, qf, kf, vf, sc, vs, need, flag, pool_k, pool_v):
    nb, n_pages = page_table.shape
    ts = qf.shape[0] // nb
    width = sc.shape[1]
    seq3 = lambda b, pt: (b, 0, 0)
    row2 = lambda b, pt: (b, 0)
    grid_spec = pltpu.PrefetchScalarGridSpec(
        num_scalar_prefetch=1,
        grid=(nb,),
        in_specs=[pl.BlockSpec((1, ts, D_ATT), seq3), pl.BlockSpec((1, ts, D_ATT), seq3),
                  pl.BlockSpec((1, ts, D_ATT), seq3), pl.BlockSpec((ts, width), row2),
                  pl.BlockSpec((ts, LANES), row2), pl.BlockSpec((ts, LANES), row2), pl.BlockSpec((ts, LANES), row2),
                  pl.BlockSpec(memory_space=pl.ANY), pl.BlockSpec(memory_space=pl.ANY)],
        out_specs=pl.BlockSpec((1, ts, D_ATT), seq3),
        scratch_shapes=[pltpu.VMEM((2, n_pages, PAGE_SIZE, D_ATT), jnp.float32),
                        pltpu.VMEM((2, n_pages, PAGE_SIZE, D_ATT), jnp.float32),
                        pltpu.SemaphoreType.DMA((2,)), pltpu.SemaphoreType.DMA((2,)),
                        pltpu.VMEM((N_HEADS * ts, width), jnp.float32)],
    )
    r3 = lambda x: x.reshape(nb, ts, D_ATT)
    out = pl.pallas_call(
        functools.partial(_dsa_sample_kernel, n_pages=n_pages),
        grid_spec=grid_spec,
        out_shape=jax.ShapeDtypeStruct((nb, ts, D_ATT), jnp.float32),
        compiler_params=pltpu.CompilerParams(dimension_semantics=("arbitrary",), vmem_limit_bytes=VMEM_LIMIT),
        name="dsa_sample",
    )(page_table, r3(qf), r3(kf), r3(vf), sc, vs, need, flag, pool_k, pool_v)
    return out.reshape(nb * ts, D_ATT)


def _split_w_in(w_in):
    wide = N_WIDE * D_CONV
    wa = w_in[:, :wide].astype(jnp.bfloat16)
    wb = jnp.pad(w_in[:, wide:], ((0, 0), (0, LANES - (w_in.shape[1] - wide)))).astype(jnp.bfloat16)
    return wa, wb


def _tile2(g):
    return jnp.concatenate([g, g] * (LANES // (2 * HEAD_DIM)), axis=-1).reshape(1, LANES)


def kernel(x_prompt, x_sample, cache_k, cache_v, cache_idx_k, state_conv_mix, state_conv_ffn, cache_mem_k, cache_mem_v, page_table, mem_prompt, g_mix, w_in, conv_mix_w, q_norm_g, k_norm_g, w_out, g_mem, g_mem_src, w_q_mem, w_kv_mem, mq_norm_g, mk_norm_g, w_o_mem, g_ffn, w_gu, conv_ffn_w, conv_ffn_b, w_down):
    depth = w_in.shape[0]
    bp, s, _ = x_prompt.shape
    bs, ts, _ = x_sample.shape
    assert ts == SUBLANES, "the sample path treats each 8-row sublane group as one sequence"
    n_pages = page_table.shape[1]
    past = n_pages * PAGE_SIZE
    n_pool = cache_k.shape[1]
    ns = bs * ts
    bf16 = jnp.bfloat16
    row = lambda v: v.reshape(1, -1)
    pad_taps = lambda w: jnp.pad(w, ((0, SUBLANES - w.shape[0]), (0, 0)))

    def prevpad(state):
        return jnp.pad(state, ((0, 0), (0, ts - state.shape[1]), (0, 0))).reshape(1, ns, state.shape[2])

    def last2(tail, groups):
        c = tail.shape[-1]
        return tail.reshape(groups, SUBLANES, c)[:, SUBLANES - 2:, :]

    pos_p = jnp.arange(s)
    pos_s = jnp.tile(past + jnp.arange(ts), bs)
    topk_s = min(TOPK_MAX, (past + ts) // 4)
    hp = x_prompt
    hs = x_sample.reshape(1, ns, D_MODEL)
    outs = [[] for _ in range(12)]
    for l in range(depth):
        wa, wb = _split_w_in(w_in[l])
        g = row(g_mix[l])
        cw = pad_taps(conv_mix_w[l])
        qg, kg = _tile2(q_norm_g[l]), _tile2(k_norm_g[l])
        wo = w_out[l].astype(bf16)
        wq = w_q_mem[l].astype(bf16)
        wom = w_o_mem[l].astype(bf16)
        wg = w_gu[l][:, :D_FF].astype(bf16)
        wu = w_gu[l][:, D_FF:].astype(bf16)
        wd = w_down[l].astype(bf16)
        cwf = pad_taps(conv_ffn_w[l])
        cbf = row(conv_ffn_b[l])
        gm, gf, mqg = row(g_mem[l]), row(g_ffn[l]), row(mq_norm_g[l])

        up, tailp, kfp, vfp, kibp, qh, qih, kt, kit, vb = _proj(hp, None, pos_p, g, wa, wb, cw, qg, kg, seq8=False)
        attp = _dsa_prompt(qh, qih, kibp, kt, kit, vb)
        mk, mv = _memkv(mem_prompt, row(g_mem_src[l]), w_kv_mem[l].astype(bf16), row(mk_norm_g[l]))
        h1p, qmp = _post_attn(hp, up, attp, wo, gm, wq, mqg, bf16)
        omp = _mem_attn(qmp, mk, mv, TILE)
        hp, tailfp = _ffn(h1p, omp, None, wom, gf, wg, wu, cwf, cbf, wd, seq8=False)

        us, tails, kfs, vfs, kibs, qfs, qifs = _proj(
            hs, prevpad(state_conv_mix[l]), pos_s, g, wa, wb, cw, qg, kg, seq8=True)
        sc = _idx_sample(page_table, qifs[0], kibs[0], cache_idx_k[l])
        vs, need, flag = _threshold(sc, topk_s)
        atts = _dsa_sample(page_table, qfs[0], kfs[0], vfs[0], sc, vs, need, flag,
                           cache_k[l].reshape(n_pool, PAGE_SIZE, D_ATT), cache_v[l].reshape(n_pool, PAGE_SIZE, D_ATT))
        h1s, qms = _post_attn(hs, us, atts.reshape(1, ns, D_ATT), wo, gm, wq, mqg, jnp.float32)
        m_tok = cache_mem_k.shape[2]
        oms = _mem_attn(qms.reshape(bs, ts, D_MODEL), cache_mem_k[l].reshape(bs, m_tok, D_MODEL),
                        cache_mem_v[l].reshape(bs, m_tok, D_MODEL), ts)
        hs, tailfs = _ffn(h1s, oms.reshape(1, ns, D_MODEL), prevpad(state_conv_ffn[l]),
                          wom, gf, wg, wu, cwf, cbf, wd, seq8=True)

        per_layer = [
            kfp.reshape(bp, s, N_HEADS, HEAD_DIM), vfp.reshape(bp, s, N_HEADS, HEAD_DIM), kibp[:, :, :IDX_DIM],
            last2(tailp, bp), last2(tailfp, bp),
            mk.reshape(bp, m_tok, MEM_HEADS, MEM_HEAD_DIM), mv.reshape(bp, m_tok, MEM_HEADS, MEM_HEAD_DIM),
            kfs.reshape(bs, ts, N_HEADS, HEAD_DIM), vfs.reshape(bs, ts, N_HEADS, HEAD_DIM),
            kibs[0, :, :IDX_DIM].reshape(bs, ts, IDX_DIM),
            last2(tails, bs), last2(tailfs, bs),
        ]
        for acc, o in zip(outs, per_layer):
            acc.append(o)
    return (hp, hs.reshape(bs, ts, D_MODEL)) + tuple(jnp.stack(o) for o in outs)
```

```python
import functools
import math

import jax
import jax.numpy as jnp
from jax import lax
from jax.experimental import pallas as pl
from jax.experimental.pallas import tpu as pltpu

D_MODEL = 1024
D_CONV = 512
N_HEADS = 8
HEAD_DIM = 64
N_IDX_HEADS = 8
IDX_DIM = 64
TOPK_MAX = 256
ROPE_THETA = 10000.0
MEM_HEADS = 4
MEM_HEAD_DIM = 256
D_FF = 2816
PAGE_SIZE = 128
EPS = 1e-6
ATTN_SCALE = HEAD_DIM ** -0.5
MEM_SCALE = MEM_HEAD_DIM ** -0.5
IDX_SCALE = (N_IDX_HEADS * IDX_DIM) ** -0.5
D_ATT = N_HEADS * HEAD_DIM
D_IDX = N_IDX_HEADS * IDX_DIM
N_WIDE = 7

LANES = 128
SUBLANES = 8
VMEM_LIMIT = 56 * 1024 * 1024

TILE = 256
INT_MIN = -(2 ** 31)


def _rms(x, g):
    ms = jnp.mean(x * x, axis=-1, keepdims=True)
    return (x * lax.rsqrt(ms + EPS)) * g


def _lane_iota(shape):
    return lax.broadcasted_iota(jnp.int32, shape, len(shape) - 1)


def _row_iota(shape):
    return lax.broadcasted_iota(jnp.int32, shape, len(shape) - 2)


def _head_rms(x, g2):
    low = _lane_iota(x.shape) < HEAD_DIM
    ss = x * x
    s_lo = jnp.sum(jnp.where(low, ss, 0.0), axis=-1, keepdims=True)
    s_hi = jnp.sum(jnp.where(low, 0.0, ss), axis=-1, keepdims=True)
    r = jnp.where(low, lax.rsqrt(s_lo * (1.0 / HEAD_DIM) + EPS), lax.rsqrt(s_hi * (1.0 / HEAD_DIM) + EPS))
    return (x * r) * g2


def _rope(x, cos, sin_signed):
    w = x.shape[-1]
    half = HEAD_DIM // 2
    fwd = pltpu.roll(x, half, axis=1)
    bwd = pltpu.roll(x, w - half, axis=1)
    first = (_lane_iota(x.shape) % HEAD_DIM) < half
    return x * cos + jnp.where(first, bwd, fwd) * sin_signed


def _shifted_rows(u, carry, prevpad, seq8):
    t = u.shape[0]
    r1 = pltpu.roll(u, 1, axis=0)
    r2 = pltpu.roll(u, 2, axis=0)
    row = _row_iota(u.shape)
    if seq8:
        in_seq = row % SUBLANES
        r1 = jnp.where(in_seq == 0, pltpu.roll(prevpad, t - 1, axis=0), r1)
        r2 = jnp.where(in_seq < 2, prevpad, r2)
    else:
        c1 = jnp.broadcast_to(carry[SUBLANES - 1:SUBLANES, :], u.shape)
        c2 = jnp.broadcast_to(carry[SUBLANES - 2:SUBLANES - 1, :], u.shape)
        r1 = jnp.where(row == 0, c1, r1)
        r2 = jnp.where(row == 0, c2, jnp.where(row == 1, c1, r2))
    return r1, r2


def _proj_kernel(*refs, seq8):
    if seq8:
        (x_ref, g_ref, wa_ref, wb_ref, cw_ref, qg_ref, kg_ref, cos_ref, sin_ref, prev_ref,
         u_ref, tail_ref, kf_ref, vf_ref, kib_ref, qf_ref, qif_ref) = refs
        carry_ref = None
    else:
        (x_ref, g_ref, wa_ref, wb_ref, cw_ref, qg_ref, kg_ref, cos_ref, sin_ref,
         u_ref, tail_ref, kf_ref, vf_ref, kib_ref, qh_ref, qih_ref, kt_ref, kit_ref, vb_ref, carry_ref) = refs

    x = x_ref[0]
    xn = _rms(x, g_ref[...]).astype(jnp.bfloat16)

    def piece(p):
        return jnp.dot(xn, wa_ref[:, p * D_CONV:(p + 1) * D_CONV], preferred_element_type=jnp.float32)

    cb, cc, ch = piece(0), piece(1), piece(2)
    u = cc * ch
    if seq8:
        r1, r2 = _shifted_rows(u, None, prev_ref[0], True)
        tail_ref[0] = u
    else:
        @pl.when(pl.program_id(1) == 0)
        def _():
            carry_ref[...] = jnp.zeros_like(carry_ref)
        r1, r2 = _shifted_rows(u, carry_ref[...], None, False)
        tail = u[u.shape[0] - SUBLANES:, :]
        carry_ref[...] = tail
        tail_ref[0] = tail
    cw = cw_ref[...]
    conv = r2 * cw[0:1, :] + r1 * cw[1:2, :] + u * cw[2:3, :]
    u_ref[0] = (cb * conv).astype(u_ref.dtype)

    cos = cos_ref[...]
    sin = sin_ref[...]
    cos4 = jnp.concatenate([cos] * (D_ATT // LANES), axis=1)
    sin4 = jnp.concatenate([sin] * (D_ATT // LANES), axis=1)

    def normed(z, g_ref_):
        g2 = g_ref_[...]
        tiles = [_head_rms(z[:, c * LANES:(c + 1) * LANES], g2) for c in range(D_ATT // LANES)]
        return jnp.concatenate(tiles, axis=1)

    q = _rope(normed(piece(3), qg_ref), cos4, sin4) * ATTN_SCALE
    k = _rope(normed(piece(4), kg_ref), cos4, sin4)
    v = piece(5)
    qi = _rope(piece(6), cos4, sin4)
    zb = jnp.dot(xn, wb_ref[...], preferred_element_type=jnp.float32)
    lane = _lane_iota(zb.shape)
    kib = jnp.where(lane < IDX_DIM, _rope(zb, cos, sin), zb * IDX_SCALE)
    kf_ref[0] = k
    vf_ref[0] = v
    kib_ref[0] = kib
    if seq8:
        qf_ref[0] = q
        qif_ref[0] = qi
    else:
        for h in range(N_HEADS):
            qh_ref[0, h] = q[:, h * HEAD_DIM:(h + 1) * HEAD_DIM].astype(jnp.bfloat16)
            qih_ref[0, h] = qi[:, h * IDX_DIM:(h + 1) * IDX_DIM].astype(jnp.bfloat16)
        kt = k.T
        for h in range(N_HEADS):
            kt_ref[0, 0, h] = kt[h * HEAD_DIM:(h + 1) * HEAD_DIM, :].astype(jnp.bfloat16)
        kit_ref[0, 0] = kib.T[:IDX_DIM, :].astype(jnp.bfloat16)
        vb_ref[0] = v.astype(jnp.bfloat16)


def _rope_tables(pos):
    half = HEAD_DIM // 2
    inv = jnp.exp(jnp.arange(half, dtype=jnp.float32) * (-2.0 * math.log(ROPE_THETA) / HEAD_DIM))
    ang = pos.astype(jnp.float32)[:, None] * inv[None, :]
    cos = jnp.cos(ang)
    sin = jnp.sin(ang)
    cos_t = jnp.concatenate([cos, cos] * (LANES // HEAD_DIM), axis=1)
    sin_t = jnp.concatenate([-sin, sin] * (LANES // HEAD_DIM), axis=1)
    return cos_t, sin_t


def _const_spec(shape):
    nd = len(shape)
    return pl.BlockSpec(shape, lambda *_: (0,) * nd)


def _proj(x, prevpad, pos, g, wa, wb, cw, qg, kg, *, seq8):
    b, s, d = x.shape
    tm = TILE
    nt = s // tm
    cos_t, sin_t = _rope_tables(pos)
    row3 = lambda bi, i: (bi, i, 0)
    in_specs = [
        pl.BlockSpec((1, tm, d), row3),
        _const_spec((1, d)),
        _const_spec(wa.shape),
        _const_spec(wb.shape),
        _const_spec(cw.shape),
        _const_spec((1, LANES)),
        _const_spec((1, LANES)),
        pl.BlockSpec((tm, LANES), lambda bi, i: (i, 0)),
        pl.BlockSpec((tm, LANES), lambda bi, i: (i, 0)),
    ]
    args = [x, g, wa, wb, cw, qg, kg, cos_t, sin_t]
    f32 = jnp.float32
    out_shape = [
        jax.ShapeDtypeStruct((b, s, D_CONV), jnp.bfloat16),
    ]
    out_specs = [pl.BlockSpec((1, tm, D_CONV), row3)]
    if seq8:
        in_specs.append(pl.BlockSpec((1, tm, D_CONV), row3))
        args.append(prevpad)
        out_shape.append(jax.ShapeDtypeStruct((b, s, D_CONV), f32))
        out_specs.append(pl.BlockSpec((1, tm, D_CONV), row3))
    else:
        out_shape.append(jax.ShapeDtypeStruct((b, SUBLANES, D_CONV), f32))
        out_specs.append(pl.BlockSpec((1, SUBLANES, D_CONV), lambda bi, i: (bi, 0, 0)))
    out_shape += [
        jax.ShapeDtypeStruct((b, s, D_ATT), f32),
        jax.ShapeDtypeStruct((b, s, D_ATT), f32),
        jax.ShapeDtypeStruct((b, s, LANES), f32),
    ]
    out_specs += [pl.BlockSpec((1, tm, D_ATT), row3), pl.BlockSpec((1, tm, D_ATT), row3),
                  pl.BlockSpec((1, tm, LANES), row3)]
    scratch = []
    if seq8:
        out_shape += [jax.ShapeDtypeStruct((b, s, D_ATT), f32), jax.ShapeDtypeStruct((b, s, D_IDX), f32)]
        out_specs += [pl.BlockSpec((1, tm, D_ATT), row3), pl.BlockSpec((1, tm, D_IDX), row3)]
    else:
        bf = jnp.bfloat16
        out_shape += [
            jax.ShapeDtypeStruct((b, N_HEADS, s, HEAD_DIM), bf),
            jax.ShapeDtypeStruct((b, N_IDX_HEADS, s, IDX_DIM), bf),
            jax.ShapeDtypeStruct((b, nt, N_HEADS, HEAD_DIM, tm), bf),
            jax.ShapeDtypeStruct((b, nt, IDX_DIM, tm), bf),
            jax.ShapeDtypeStruct((b, s, D_ATT), bf),
        ]
        out_specs += [
            pl.BlockSpec((1, N_HEADS, tm, HEAD_DIM), lambda bi, i: (bi, 0, i, 0)),
            pl.BlockSpec((1, N_IDX_HEADS, tm, IDX_DIM), lambda bi, i: (bi, 0, i, 0)),
            pl.BlockSpec((1, 1, N_HEADS, HEAD_DIM, tm), lambda bi, i: (bi, i, 0, 0, 0)),
            pl.BlockSpec((1, 1, IDX_DIM, tm), lambda bi, i: (bi, i, 0, 0)),
            pl.BlockSpec((1, tm, D_ATT), row3),
        ]
        scratch = [pltpu.VMEM((SUBLANES, D_CONV), f32)]
    outs = pl.pallas_call(
        functools.partial(_proj_kernel, seq8=seq8),
        grid=(b, nt),
        in_specs=in_specs,
        out_specs=out_specs,
        out_shape=out_shape,
        scratch_shapes=scratch,
        compiler_params=pltpu.CompilerParams(
            dimension_semantics=("arbitrary", "arbitrary"), vmem_limit_bytes=VMEM_LIMIT),
        name="proj_sample" if seq8 else "proj_prompt",
    )(*args)
    return outs


NEG = -1e30
BIS_ROWS = 128


def _sortable(x):
    b = pltpu.bitcast(x, jnp.int32)
    return b ^ ((b >> 31) & 0x7FFFFFFF)


def _rep(x, n):
    return x if n == 1 else jnp.concatenate([x] * n, axis=1)


def _kth_largest_key(load_block, n_blocks, block_w, rows, k):
    def step(it, prefix):
        cand = prefix + jnp.left_shift(jnp.int32(1), 31 - it)

        def count(j, cnt):
            blk = load_block(j)
            for c in range(block_w // LANES):
                cnt = cnt + jnp.where(blk[:, c * LANES:(c + 1) * LANES] >= cand, 1, 0)
            return cnt

        cnt = lax.fori_loop(0, n_blocks, count, jnp.zeros((rows, LANES), jnp.int32))
        total = jnp.sum(cnt, axis=-1, keepdims=True)
        return jnp.where(total >= k, cand, prefix)

    return lax.fori_loop(0, 32, step, jnp.full((rows, LANES), INT_MIN, jnp.int32))


def _count_ge_gt(load_block, n_blocks, block_w, rows, v):
    def count(j, carry):
        ge, gt = carry
        blk = load_block(j)
        for c in range(block_w // LANES):
            t = blk[:, c * LANES:(c + 1) * LANES]
            ge = ge + jnp.where(t >= v, 1, 0)
            gt = gt + jnp.where(t > v, 1, 0)
        return ge, gt

    z = jnp.zeros((rows, LANES), jnp.int32)
    ge, gt = lax.fori_loop(0, n_blocks, count, (z, z))
    return jnp.sum(ge, axis=-1, keepdims=True), jnp.sum(gt, axis=-1, keepdims=True)


def _dsa_prompt_kernel(qh_ref, qih_ref, kib_ref, kt_ref, kit_ref, vb_ref, o_ref,
                       sc_ref, wb_ref, vs_ref, need_ref, tie_ref, m_ref, l_ref, acc_ref, *, topk):
    i = pl.program_id(1)
    tq = tk = TILE
    nl = tk // LANES
    f32 = jnp.float32

    for h in range(N_IDX_HEADS):
        wb_ref[h] = jnp.broadcast_to(kib_ref[0, :, IDX_DIM + h:IDX_DIM + h + 1], (tq, LANES))

    def score_block(j, diagonal):
        kit = kit_ref[0, j]
        acc = jnp.zeros((tq, tk), f32)
        for h in range(N_IDX_HEADS):
            d = jnp.dot(qih_ref[0, h], kit, preferred_element_type=f32)
            acc = acc + jnp.maximum(d, 0.0) * _rep(wb_ref[h], nl)
        key = _sortable(acc)
        if diagonal:
            key = jnp.where(_lane_iota(key.shape) > _row_iota(key.shape), INT_MIN, key)
        sc_ref[j] = key

    def score_body(j, c):
        score_block(j, False)
        return c

    lax.fori_loop(0, i, score_body, 0)
    score_block(i, True)

    flag = jnp.int32(0)
    for rb in range(tq // BIS_ROWS):
        rows = slice(rb * BIS_ROWS, (rb + 1) * BIS_ROWS)
        load = lambda j: sc_ref[j, rows, :]
        vstar = _kth_largest_key(load, i + 1, tk, BIS_ROWS, topk)
        vs = jnp.maximum(vstar, INT_MIN + 1)
        n_ge, n_gt = _count_ge_gt(load, i + 1, tk, BIS_ROWS, vs)
        vs_ref[rows, :] = vs
        need_ref[rows, :] = jnp.broadcast_to((topk - n_gt).astype(f32), (BIS_ROWS, LANES))
        flag = jnp.maximum(flag, jnp.max(jnp.where(n_ge > topk, 1, 0)))

    m_ref[...] = jnp.full(m_ref.shape, NEG, f32)
    l_ref[...] = jnp.zeros(l_ref.shape, f32)
    acc_ref[...] = jnp.zeros(acc_ref.shape, f32)
    tie_ref[...] = jnp.zeros(tie_ref.shape, f32)

    def attend(j, ties):
        key = sc_ref[j]
        vs = _rep(vs_ref[...], nl)
        if ties:
            eq = key == vs
            upper = (_row_iota((tk, tk)) <= _lane_iota((tk, tk))).astype(jnp.bfloat16)
            e = jnp.where(eq, 1.0, 0.0)
            rank = jnp.dot(e.astype(jnp.bfloat16), upper, preferred_element_type=f32) + _rep(tie_ref[...], nl)
            tie_ref[...] = tie_ref[...] + jnp.sum(e, axis=-1, keepdims=True)
            mask = (key > vs) | (eq & (rank <= _rep(need_ref[...], nl)))
        else:
            mask = key >= vs
        for h in range(N_HEADS):
            s = jnp.dot(qh_ref[0, h], kt_ref[0, j, h], preferred_element_type=f32)
            sm = jnp.where(mask, s, NEG)
            m_old = m_ref[h]
            m_new = jnp.maximum(m_old, jnp.max(sm, axis=-1, keepdims=True))
            alpha = jnp.exp(m_old - m_new)
            p = jnp.exp(sm - _rep(m_new, nl))
            l_ref[h] = alpha * l_ref[h] + jnp.sum(p, axis=-1, keepdims=True)
            c = h // 2
            pv = jnp.dot(p.astype(jnp.bfloat16), vb_ref[0, j, :, c * LANES:(c + 1) * LANES],
                         preferred_element_type=f32)
            acc_ref[h] = alpha * acc_ref[h] + pv
            m_ref[h] = m_new

    def run(ties):
        def body(j, c):
            attend(j, ties)
            return c
        lax.fori_loop(0, i + 1, body, 0)

    @pl.when(flag == 0)
    def _():
        run(False)

    @pl.when(flag != 0)
    def _():
        run(True)

    low = _lane_iota((tq, LANES)) < HEAD_DIM
    for c in range(N_HEADS // 2):
        even = acc_ref[2 * c] / l_ref[2 * c]
        odd = acc_ref[2 * c + 1] / l_ref[2 * c + 1]
        o_ref[0, :, c * LANES:(c + 1) * LANES] = jnp.where(low, even, odd).astype(o_ref.dtype)


def _dsa_prompt(qh, qih, kib, kt, kit, vb):
    b, _, s, _ = qh.shape
    t = TILE
    nt = s // t
    topk = min(TOPK_MAX, s // 4)
    vb4 = vb.reshape(b, nt, t, D_ATT)
    resident = dict(pipeline_mode=pl.Buffered(1))
    return pl.pallas_call(
        functools.partial(_dsa_prompt_kernel, topk=topk),
        grid=(b, nt),
        in_specs=[
            pl.BlockSpec((1, N_HEADS, t, HEAD_DIM), lambda bi, i: (bi, 0, i, 0)),
            pl.BlockSpec((1, N_IDX_HEADS, t, IDX_DIM), lambda bi, i: (bi, 0, i, 0)),
            pl.BlockSpec((1, t, LANES), lambda bi, i: (bi, i, 0)),
            pl.BlockSpec((1, nt, N_HEADS, HEAD_DIM, t), lambda bi, i: (bi, 0, 0, 0, 0), **resident),
            pl.BlockSpec((1, nt, IDX_DIM, t), lambda bi, i: (bi, 0, 0, 0), **resident),
            pl.BlockSpec((1, nt, t, D_ATT), lambda bi, i: (bi, 0, 0, 0), **resident),
        ],
        out_specs=pl.BlockSpec((1, t, D_ATT), lambda bi, i: (bi, i, 0)),
        out_shape=jax.ShapeDtypeStruct((b, s, D_ATT), jnp.bfloat16),
        scratch_shapes=[
            pltpu.VMEM((nt, t, t), jnp.int32),
            pltpu.VMEM((N_IDX_HEADS, t, LANES), jnp.float32),
            pltpu.VMEM((t, LANES), jnp.int32),
            pltpu.VMEM((t, LANES), jnp.float32),
            pltpu.VMEM((t, LANES), jnp.float32),
            pltpu.VMEM((N_HEADS, t, LANES), jnp.float32),
            pltpu.VMEM((N_HEADS, t, LANES), jnp.float32),
            pltpu.VMEM((N_HEADS, t, LANES), jnp.float32),
        ],
        compiler_params=pltpu.CompilerParams(
            dimension_semantics=("arbitrary", "arbitrary"), vmem_limit_bytes=VMEM_LIMIT),
        name="dsa_prompt",
    )(qh, qih, kib, kt, kit, vb4)


def _mem_head_rms(x, g_ref):
    g = g_ref[...]
    heads = [_rms(x[:, h * MEM_HEAD_DIM:(h + 1) * MEM_HEAD_DIM], g) for h in range(MEM_HEADS)]
    return jnp.concatenate(heads, axis=1)


def _memkv_kernel(mem_ref, g_ref, w_ref, kg_ref, mk_ref, mv_ref):
    xn = _rms(mem_ref[0], g_ref[...]).astype(jnp.bfloat16)
    kg = kg_ref[...]
    for h in range(MEM_HEADS):
        cols = slice(h * MEM_HEAD_DIM, (h + 1) * MEM_HEAD_DIM)
        vcols = slice((MEM_HEADS + h) * MEM_HEAD_DIM, (MEM_HEADS + h + 1) * MEM_HEAD_DIM)
        mk_ref[0, :, h, :] = _rms(jnp.dot(xn, w_ref[:, cols], preferred_element_type=jnp.float32), kg)
        mv_ref[0, :, h, :] = jnp.dot(xn, w_ref[:, vcols], preferred_element_type=jnp.float32)


def _memkv(mem, g, w_kv, kg):
    b, m, d = mem.shape
    blk = pl.BlockSpec((1, m, MEM_HEADS, MEM_HEAD_DIM), lambda bi: (bi, 0, 0, 0))
    return pl.pallas_call(
        _memkv_kernel,
        grid=(b,),
        in_specs=[pl.BlockSpec((1, m, d), lambda bi: (bi, 0, 0)), _const_spec((1, d)),
                  _const_spec(w_kv.shape), _const_spec((1, MEM_HEAD_DIM))],
        out_specs=[blk, blk],
        out_shape=[jax.ShapeDtypeStruct((b, m, MEM_HEADS, MEM_HEAD_DIM), jnp.float32)] * 2,
        compiler_params=pltpu.CompilerParams(dimension_semantics=("arbitrary",), vmem_limit_bytes=VMEM_LIMIT),
        name="memory_kv",
    )(mem, g, w_kv, kg)


def _post_attn_kernel(x_ref, u_ref, att_ref, wo_ref, g_ref, wq_ref, qg_ref, h_ref, qm_ref):
    y = jnp.dot(u_ref[0].astype(jnp.bfloat16), wo_ref[:D_CONV, :], preferred_element_type=jnp.float32)
    y = y + jnp.dot(att_ref[0].astype(jnp.bfloat16), wo_ref[D_CONV:, :], preferred_element_type=jnp.float32)
    h = x_ref[0] + y
    h_ref[0] = h
    q = jnp.dot(_rms(h, g_ref[...]).astype(jnp.bfloat16), wq_ref[...], preferred_element_type=jnp.float32)
    qm_ref[0] = (_mem_head_rms(q, qg_ref) * MEM_SCALE).astype(qm_ref.dtype)


def _post_attn(x, u, att, wo, g, wq, qg, qm_dtype):
    b, s, d = x.shape
    tm = TILE
    row3 = lambda bi, i: (bi, i, 0)
    return pl.pallas_call(
        _post_attn_kernel,
        grid=(b, s // tm),
        in_specs=[pl.BlockSpec((1, tm, d), row3), pl.BlockSpec((1, tm, D_CONV), row3),
                  pl.BlockSpec((1, tm, D_ATT), row3), _const_spec(wo.shape), _const_spec((1, d)),
                  _const_spec(wq.shape), _const_spec((1, MEM_HEAD_DIM))],
        out_specs=[pl.BlockSpec((1, tm, d), row3), pl.BlockSpec((1, tm, d), row3)],
        out_shape=[jax.ShapeDtypeStruct((b, s, d), jnp.float32), jax.ShapeDtypeStruct((b, s, d), qm_dtype)],
        compiler_params=pltpu.CompilerParams(
            dimension_semantics=("arbitrary", "arbitrary"), vmem_limit_bytes=VMEM_LIMIT),
        name="post_attn",
    )(x, u, att, wo, g, wq, qg)


def _mem_attn_kernel(q_ref, mk_ref, mv_ref, o_ref):
    q = q_ref[0].astype(jnp.bfloat16)
    for h in range(MEM_HEADS):
        cols = slice(h * MEM_HEAD_DIM, (h + 1) * MEM_HEAD_DIM)
        logits = lax.dot_general(q[:, cols], mk_ref[0, :, h, :].astype(jnp.bfloat16), NT_DIMS,
                                 preferred_element_type=jnp.float32)
        m = jnp.max(logits, axis=-1, keepdims=True)
        p = jnp.exp(logits - m)
        l = jnp.sum(p, axis=-1, keepdims=True)
        o = jnp.dot(p.astype(jnp.bfloat16), mv_ref[0, :, h, :].astype(jnp.bfloat16),
                    preferred_element_type=jnp.float32)
        o_ref[0, :, cols] = (o / l).astype(o_ref.dtype)


def _mem_attn(qm, mk, mv, tm):
    g, s, d = qm.shape
    m = mk.shape[1]
    row3 = lambda gi, i: (gi, i, 0)
    grp = pl.BlockSpec((1, m, MEM_HEADS, MEM_HEAD_DIM), lambda gi, i: (gi, 0, 0, 0))
    return pl.pallas_call(
        _mem_attn_kernel,
        grid=(g, s // tm),
        in_specs=[pl.BlockSpec((1, tm, d), row3), grp, grp],
        out_specs=pl.BlockSpec((1, tm, d), row3),
        out_shape=jax.ShapeDtypeStruct((g, s, d), qm.dtype),
        compiler_params=pltpu.CompilerParams(
            dimension_semantics=("arbitrary", "arbitrary"), vmem_limit_bytes=VMEM_LIMIT),
        name="mem_attn",
    )(qm, mk, mv)


def _ffn_kernel(*refs, seq8):
    if seq8:
        h_ref, o_ref, wo_ref, g_ref, wg_ref, wu_ref, cw_ref, cb_ref, wd_ref, prev_ref, y_ref, tail_ref = refs
        carry_ref = None
    else:
        h_ref, o_ref, wo_ref, g_ref, wg_ref, wu_ref, cw_ref, cb_ref, wd_ref, y_ref, tail_ref, carry_ref = refs
    h = h_ref[0] + jnp.dot(o_ref[0].astype(jnp.bfloat16), wo_ref[...], preferred_element_type=jnp.float32)
    xn = _rms(h, g_ref[...]).astype(jnp.bfloat16)
    gate = jnp.dot(xn, wg_ref[...], preferred_element_type=jnp.float32)
    up = jnp.dot(xn, wu_ref[...], preferred_element_type=jnp.float32)
    if seq8:
        r1, r2 = _shifted_rows(gate, None, prev_ref[0], True)
        tail_ref[0] = gate
    else:
        @pl.when(pl.program_id(1) == 0)
        def _():
            carry_ref[...] = jnp.zeros_like(carry_ref)
        r1, r2 = _shifted_rows(gate, carry_ref[...], None, False)
        tail = gate[gate.shape[0] - SUBLANES:, :]
        carry_ref[...] = tail
        tail_ref[0] = tail
    cw = cw_ref[...]
    gc = r2 * cw[0:1, :] + r1 * cw[1:2, :] + gate * cw[2:3, :] + cb_ref[...]
    act = (gc / (1.0 + jnp.exp(-gc))) * up
    y_ref[0] = h + jnp.dot(act.astype(jnp.bfloat16), wd_ref[...], preferred_element_type=jnp.float32)


def _ffn(h, o, prevpad, wo, g, wg, wu, cw, cb, wd, *, seq8):
    b, s, d = h.shape
    tm = TILE
    row3 = lambda bi, i: (bi, i, 0)
    resident = dict(pipeline_mode=pl.Buffered(1))
    wspec = lambda w: pl.BlockSpec(w.shape, lambda bi, i: (0, 0), **resident)
    in_specs = [pl.BlockSpec((1, tm, d), row3), pl.BlockSpec((1, tm, d), row3), wspec(wo), _const_spec((1, d)),
                wspec(wg), wspec(wu), _const_spec(cw.shape), _const_spec((1, D_FF)), wspec(wd)]
    args = [h, o, wo, g, wg, wu, cw, cb, wd]
    out_shape = [jax.ShapeDtypeStruct((b, s, d), jnp.float32)]
    out_specs = [pl.BlockSpec((1, tm, d), row3)]
    scratch = []
    if seq8:
        in_specs.append(pl.BlockSpec((1, tm, D_FF), row3))
        args.append(prevpad)
        out_shape.append(jax.ShapeDtypeStruct((b, s, D_FF), jnp.float32))
        out_specs.append(pl.BlockSpec((1, tm, D_FF), row3))
    else:
        out_shape.append(jax.ShapeDtypeStruct((b, SUBLANES, D_FF), jnp.float32))
        out_specs.append(pl.BlockSpec((1, SUBLANES, D_FF), lambda bi, i: (bi, 0, 0)))
        scratch = [pltpu.VMEM((SUBLANES, D_FF), jnp.float32)]
    return pl.pallas_call(
        functools.partial(_ffn_kernel, seq8=seq8),
        grid=(b, s // tm),
        in_specs=in_specs,
        out_specs=out_specs,
        out_shape=out_shape,
        scratch_shapes=scratch,
        compiler_params=pltpu.CompilerParams(
            dimension_semantics=("arbitrary", "arbitrary"), vmem_limit_bytes=VMEM_LIMIT),
        name="ffn_sample" if seq8 else "ffn_prompt",
    )(*args)


def _page_copies(pt_ref, b, n_pages, pool_ref, buf_ref, sem_ref, slot):
    return [pltpu.make_async_copy(pool_ref.at[pt_ref[b, p]], buf_ref.at[slot, p], sem_ref.at[slot])
            for p in range(n_pages)]


def _paged_fetch(pt_ref, n_pages, pools, bufs, sems):
    b = pl.program_id(0)
    nb = pl.num_programs(0)
    slot = b % 2

    def start(seq, s):
        for pool, buf, sem in zip(pools, bufs, sems):
            for cp in _page_copies(pt_ref, seq, n_pages, pool, buf, sem, s):
                cp.start()

    @pl.when(b == 0)
    def _():
        start(b, slot)

    @pl.when(b + 1 < nb)
    def _():
        start(b + 1, 1 - slot)

    for pool, buf, sem in zip(pools, bufs, sems):
        for cp in _page_copies(pt_ref, b, n_pages, pool, buf, sem, slot):
            cp.wait()
    return slot


def _stack_heads(x, width):
    return jnp.concatenate([x[:, h * width:(h + 1) * width] for h in range(x.shape[1] // width)], axis=0)


def _pad_rows(x, rows):
    return jnp.concatenate([x, jnp.zeros((rows - x.shape[0], x.shape[1]), x.dtype)], axis=0)


NT_DIMS = (((1,), (1,)), ((), ()))
NN_DIMS = (((1,), (0,)), ((), ()))


def _idx_sample_kernel(pt_ref, qi_ref, kib_ref, pool_ref, sc_ref, buf_ref, sem_ref, *, n_pages):
    slot = _paged_fetch(pt_ref, n_pages, [pool_ref], [buf_ref], [sem_ref])
    f32 = jnp.float32
    ts = qi_ref.shape[1]
    kib = kib_ref[0]
    qis = _stack_heads(qi_ref[0], IDX_DIM).astype(jnp.bfloat16)
    w = [jnp.broadcast_to(kib[:, IDX_DIM + h:IDX_DIM + h + 1], (ts, LANES)) for h in range(N_IDX_HEADS)]

    def scores(keys, dims):
        d = lax.dot_general(qis, keys.astype(jnp.bfloat16), dims, preferred_element_type=f32)
        acc = jnp.zeros((ts, LANES), f32)
        for h in range(N_IDX_HEADS):
            acc = acc + jnp.maximum(d[h * ts:(h + 1) * ts, :], 0.0) * w[h]
        return _sortable(acc)

    for p in range(n_pages):
        sc_ref[:, p * PAGE_SIZE:(p + 1) * PAGE_SIZE] = scores(buf_ref[slot, p], NN_DIMS)
    new = scores(_pad_rows(kib[:, :IDX_DIM], LANES), NT_DIMS)
    causal = _lane_iota((ts, LANES)) <= _row_iota((ts, LANES))
    sc_ref[:, n_pages * PAGE_SIZE:] = jnp.where(causal, new, INT_MIN)


def _idx_sample(page_table, qif, kib, pool_idx):
    nb, n_pages = page_table.shape
    ts = qif.shape[0] // nb
    width = n_pages * PAGE_SIZE + LANES
    grid_spec = pltpu.PrefetchScalarGridSpec(
        num_scalar_prefetch=1,
        grid=(nb,),
        in_specs=[pl.BlockSpec((1, ts, D_IDX), lambda b, pt: (b, 0, 0)),
                  pl.BlockSpec((1, ts, LANES), lambda b, pt: (b, 0, 0)),
                  pl.BlockSpec(memory_space=pl.ANY)],
        out_specs=pl.BlockSpec((ts, width), lambda b, pt: (b, 0)),
        scratch_shapes=[pltpu.VMEM((2, n_pages, IDX_DIM, PAGE_SIZE), jnp.float32),
                        pltpu.SemaphoreType.DMA((2,))],
    )
    return pl.pallas_call(
        functools.partial(_idx_sample_kernel, n_pages=n_pages),
        grid_spec=grid_spec,
        out_shape=jax.ShapeDtypeStruct((nb * ts, width), jnp.int32),
        compiler_params=pltpu.CompilerParams(dimension_semantics=("arbitrary",), vmem_limit_bytes=VMEM_LIMIT),
        name="idx_sample",
    )(page_table, qif.reshape(nb, ts, D_IDX), kib.reshape(nb, ts, LANES), pool_idx)


def _threshold_kernel(sc_ref, vs_ref, need_ref, flag_ref, *, topk):
    rows, width = sc_ref.shape
    load = lambda j: sc_ref[...]
    vs = jnp.maximum(_kth_largest_key(load, 1, width, rows, topk), INT_MIN + 1)
    n_ge, n_gt = _count_ge_gt(load, 1, width, rows, vs)
    vs_ref[...] = vs
    need_ref[...] = jnp.broadcast_to((topk - n_gt).astype(jnp.float32), (rows, LANES))
    flag_ref[...] = jnp.broadcast_to(jnp.where(n_ge > topk, 1, 0), (rows, LANES))


def _threshold(sc, topk):
    n, width = sc.shape
    rows = BIS_ROWS
    blk = pl.BlockSpec((rows, LANES), lambda i: (i, 0))
    return pl.pallas_call(
        functools.partial(_threshold_kernel, topk=topk),
        grid=(n // rows,),
        in_specs=[pl.BlockSpec((rows, width), lambda i: (i, 0))],
        out_specs=[blk, blk, blk],
        out_shape=[jax.ShapeDtypeStruct((n, LANES), jnp.int32), jax.ShapeDtypeStruct((n, LANES), jnp.float32),
                   jax.ShapeDtypeStruct((n, LANES), jnp.int32)],
        compiler_params=pltpu.CompilerParams(dimension_semantics=("arbitrary",), vmem_limit_bytes=VMEM_LIMIT),
        name="threshold_sample",
    )(sc)


def _dsa_sample_kernel(pt_ref, q_ref, kn_ref, vn_ref, sc_ref, vs_ref, need_ref, flag_ref, kpool_ref, vpool_ref,
                       o_ref, kbuf_ref, vbuf_ref, ksem_ref, vsem_ref, lg_ref, *, n_pages):
    slot = _paged_fetch(pt_ref, n_pages, [kpool_ref, vpool_ref], [kbuf_ref, vbuf_ref], [ksem_ref, vsem_ref])
    f32 = jnp.float32
    bf16 = jnp.bfloat16
    ts = q_ref.shape[1]
    nblk = n_pages + 1
    q = q_ref[0]
    lane = _lane_iota(q.shape)
    head_of_lane = lane // HEAD_DIM
    qx = jnp.concatenate([jnp.where(head_of_lane == h, q, 0.0) for h in range(N_HEADS)], axis=0).astype(bf16)

    key = sc_ref[...]
    vs = vs_ref[...]
    has_ties = jnp.max(flag_ref[...]) > 0

    def tile_h(x):
        return jnp.concatenate([x] * N_HEADS, axis=0)

    def plain_mask(c):
        return jnp.where(key[:, c * LANES:(c + 1) * LANES] >= vs, 1, 0)

    def store_logits(masks):
        for c in range(nblk):
            if c < n_pages:
                s = jnp.dot(qx, kbuf_ref[slot, c].astype(bf16), preferred_element_type=f32)
            else:
                s = lax.dot_general(qx, _pad_rows(kn_ref[0], LANES).astype(bf16), NT_DIMS,
                                    preferred_element_type=f32)
            lg_ref[:, c * LANES:(c + 1) * LANES] = jnp.where(tile_h(masks(c)) > 0, s, NEG)

    @pl.when(jnp.logical_not(has_ties))
    def _():
        store_logits(plain_mask)

    @pl.when(has_ties)
    def _():
        upper = (_row_iota((LANES, LANES)) <= _lane_iota((LANES, LANES))).astype(bf16)
        need = need_ref[...]
        seen = jnp.zeros((ts, LANES), f32)
        masks = []
        for c in range(nblk):
            kc = key[:, c * LANES:(c + 1) * LANES]
            eq = kc == vs
            e = jnp.where(eq, 1.0, 0.0)
            rank = jnp.dot(e.astype(bf16), upper, preferred_element_type=f32) + seen
            seen = seen + jnp.sum(e, axis=-1, keepdims=True)
            masks.append(jnp.where((kc > vs) | (eq & (rank <= need)), 1, 0))
        store_logits(lambda c: masks[c])

    lg = lg_ref[...]
    m = jnp.max(lg, axis=-1, keepdims=True)
    p = jnp.exp(lg - m)
    l = jnp.sum(p, axis=-1, keepdims=True)
    pb = p.astype(bf16)
    acc = jnp.zeros((N_HEADS * ts, D_ATT), f32)
    for c in range(nblk):
        pc = pb[:, c * LANES:(c + 1) * LANES]
        if c < n_pages:
            acc = acc + lax.dot_general(pc, vbuf_ref[slot, c].astype(bf16), NT_DIMS, preferred_element_type=f32)
        else:
            acc = acc + jnp.dot(pc, _pad_rows(vn_ref[0], LANES).astype(bf16), preferred_element_type=f32)
    acc = acc / l
    out = jnp.zeros((ts, D_ATT), f32)
    for h in range(N_HEADS):
        out = out + jnp.where(head_of_lane == h, acc[h * ts:(h + 1) * ts, :], 0.0)
    o_ref[0] = out


def _dsa_sample(page_table, qf, kf, vf, sc, vs, need, flag, pool_k, pool_v):
    nb, n_pages = page_table.shape
    ts = qf.shape[0] // nb
    width = sc.shape[1]
    seq3 = lambda b, pt: (b, 0, 0)
    row2 = lambda b, pt: (b, 0)
    grid_spec = pltpu.PrefetchScalarGridSpec(
        num_scalar_prefetch=1,
        grid=(nb,),
        in_specs=[pl.BlockSpec((1, ts, D_ATT), seq3), pl.BlockSpec((1, ts, D_ATT), seq3),
                  pl.BlockSpec((1, ts, D_ATT), seq3), pl.BlockSpec((ts, width), row2),
                  pl.BlockSpec((ts, LANES), row2), pl.BlockSpec((ts, LANES), row2), pl.BlockSpec((ts, LANES), row2),
                  pl.BlockSpec(memory_space=pl.ANY), pl.BlockSpec(memory_space=pl.ANY)],
        out_specs=pl.BlockSpec((1, ts, D_ATT), seq3),
        scratch_shapes=[pltpu.VMEM((2, n_pages, D_ATT, PAGE_SIZE), jnp.float32),
                        pltpu.VMEM((2, n_pages, D_ATT, PAGE_SIZE), jnp.float32),
                        pltpu.SemaphoreType.DMA((2,)), pltpu.SemaphoreType.DMA((2,)),
                        pltpu.VMEM((N_HEADS * ts, width), jnp.float32)],
    )
    r3 = lambda x: x.reshape(nb, ts, D_ATT)
    out = pl.pallas_call(
        functools.partial(_dsa_sample_kernel, n_pages=n_pages),
        grid_spec=grid_spec,
        out_shape=jax.ShapeDtypeStruct((nb, ts, D_ATT), jnp.float32),
        compiler_params=pltpu.CompilerParams(dimension_semantics=("arbitrary",), vmem_limit_bytes=VMEM_LIMIT),
        name="dsa_sample",
    )(page_table, r3(qf), r3(kf), r3(vf), sc, vs, need, flag, pool_k, pool_v)
    return out.reshape(nb * ts, D_ATT)


def _split_w_in(w_in):
    wide = N_WIDE * D_CONV
    wa = w_in[:, :wide].astype(jnp.bfloat16)
    wb = jnp.pad(w_in[:, wide:], ((0, 0), (0, LANES - (w_in.shape[1] - wide)))).astype(jnp.bfloat16)
    return wa, wb


def _tile2(g):
    return jnp.concatenate([g, g] * (LANES // (2 * HEAD_DIM)), axis=-1).reshape(1, LANES)


def kernel(x_prompt, x_sample, cache_k, cache_v, cache_idx_k, state_conv_mix, state_conv_ffn, cache_mem_k, cache_mem_v, page_table, mem_prompt, g_mix, w_in, conv_mix_w, q_norm_g, k_norm_g, w_out, g_mem, g_mem_src, w_q_mem, w_kv_mem, mq_norm_g, mk_norm_g, w_o_mem, g_ffn, w_gu, conv_ffn_w, conv_ffn_b, w_down):
    depth = w_in.shape[0]
    bp, s, _ = x_prompt.shape
    bs, ts, _ = x_sample.shape
    assert ts == SUBLANES, "the sample path treats each 8-row sublane group as one sequence"
    n_pages = page_table.shape[1]
    past = n_pages * PAGE_SIZE
    n_pool = cache_k.shape[1]
    ns = bs * ts
    bf16 = jnp.bfloat16
    row = lambda v: v.reshape(1, -1)
    pad_taps = lambda w: jnp.pad(w, ((0, SUBLANES - w.shape[0]), (0, 0)))

    def prevpad(state):
        return jnp.pad(state, ((0, 0), (0, ts - state.shape[1]), (0, 0))).reshape(1, ns, state.shape[2])

    def last2(tail, groups):
        c = tail.shape[-1]
        return tail.reshape(groups, SUBLANES, c)[:, SUBLANES - 2:, :]

    pos_p = jnp.arange(s)
    pos_s = jnp.tile(past + jnp.arange(ts), bs)
    topk_s = min(TOPK_MAX, (past + ts) // 4)
    hp = x_prompt
    hs = x_sample.reshape(1, ns, D_MODEL)
    outs = [[] for _ in range(12)]
    for l in range(depth):
        wa, wb = _split_w_in(w_in[l])
        g = row(g_mix[l])
        cw = pad_taps(conv_mix_w[l])
        qg, kg = _tile2(q_norm_g[l]), _tile2(k_norm_g[l])
        wo = w_out[l].astype(bf16)
        wq = w_q_mem[l].astype(bf16)
        wom = w_o_mem[l].astype(bf16)
        wg = w_gu[l][:, :D_FF].astype(bf16)
        wu = w_gu[l][:, D_FF:].astype(bf16)
        wd = w_down[l].astype(bf16)
        cwf = pad_taps(conv_ffn_w[l])
        cbf = row(conv_ffn_b[l])
        gm, gf, mqg = row(g_mem[l]), row(g_ffn[l]), row(mq_norm_g[l])

        up, tailp, kfp, vfp, kibp, qh, qih, kt, kit, vb = _proj(hp, None, pos_p, g, wa, wb, cw, qg, kg, seq8=False)
        attp = _dsa_prompt(qh, qih, kibp, kt, kit, vb)
        mk, mv = _memkv(mem_prompt, row(g_mem_src[l]), w_kv_mem[l].astype(bf16), row(mk_norm_g[l]))
        h1p, qmp = _post_attn(hp, up, attp, wo, gm, wq, mqg, bf16)
        omp = _mem_attn(qmp, mk, mv, TILE)
        hp, tailfp = _ffn(h1p, omp, None, wom, gf, wg, wu, cwf, cbf, wd, seq8=False)

        us, tails, kfs, vfs, kibs, qfs, qifs = _proj(
            hs, prevpad(state_conv_mix[l]), pos_s, g, wa, wb, cw, qg, kg, seq8=True)
        pool_ki = jnp.transpose(cache_idx_k[l], (0, 2, 1))
        pool_k = jnp.transpose(cache_k[l], (0, 2, 3, 1)).reshape(n_pool, D_ATT, PAGE_SIZE)
        pool_v = jnp.transpose(cache_v[l], (0, 2, 3, 1)).reshape(n_pool, D_ATT, PAGE_SIZE)
        sc = _idx_sample(page_table, qifs[0], kibs[0], pool_ki)
        vs, need, flag = _threshold(sc, topk_s)
        atts = _dsa_sample(page_table, qfs[0], kfs[0], vfs[0], sc, vs, need, flag,
                           pool_k, pool_v)
        h1s, qms = _post_attn(hs, us, atts.reshape(1, ns, D_ATT), wo, gm, wq, mqg, jnp.float32)
        oms = _mem_attn(qms.reshape(bs, ts, D_MODEL), cache_mem_k[l], cache_mem_v[l], ts)
        hs, tailfs = _ffn(h1s, oms.reshape(1, ns, D_MODEL), prevpad(state_conv_ffn[l]),
                          wom, gf, wg, wu, cwf, cbf, wd, seq8=True)

        per_layer = [
            kfp.reshape(bp, s, N_HEADS, HEAD_DIM), vfp.reshape(bp, s, N_HEADS, HEAD_DIM), kibp[:, :, :IDX_DIM],
            last2(tailp, bp), last2(tailfp, bp),
            mk, mv,
            kfs.reshape(bs, ts, N_HEADS, HEAD_DIM), vfs.reshape(bs, ts, N_HEADS, HEAD_DIM),
            kibs[0, :, :IDX_DIM].reshape(bs, ts, IDX_DIM),
            last2(tails, bs), last2(tailfs, bs),
        ]
        for acc, o in zip(outs, per_layer):
            acc.append(o)
    return (hp, hs.reshape(bs, ts, D_MODEL)) + tuple(jnp.stack(o) for o in outs)
```

```python
import functools
import math

import jax
import jax.numpy as jnp
from jax import lax
from jax.experimental import pallas as pl
from jax.experimental.pallas import tpu as pltpu

D_MODEL = 1024
D_CONV = 512
N_HEADS = 8
HEAD_DIM = 64
N_IDX_HEADS = 8
IDX_DIM = 64
TOPK_MAX = 256
ROPE_THETA = 10000.0
MEM_HEADS = 4
MEM_HEAD_DIM = 256
D_FF = 2816
PAGE_SIZE = 128
EPS = 1e-6
ATTN_SCALE = HEAD_DIM ** -0.5
Q_SCALE = ATTN_SCALE * math.log2(math.e)
MEM_SCALE = MEM_HEAD_DIM ** -0.5
IDX_SCALE = (N_IDX_HEADS * IDX_DIM) ** -0.5
D_ATT = N_HEADS * HEAD_DIM
D_IDX = N_IDX_HEADS * IDX_DIM
N_WIDE = 7

LANES = 128
SUBLANES = 8
VMEM_LIMIT = 56 * 1024 * 1024

TILE = 256
INT_MIN = -(2 ** 31)


def _rms(x, g):
    ms = jnp.mean(x * x, axis=-1, keepdims=True)
    return (x * lax.rsqrt(ms + EPS)) * g


def _lane_iota(shape):
    return lax.broadcasted_iota(jnp.int32, shape, len(shape) - 1)


def _row_iota(shape):
    return lax.broadcasted_iota(jnp.int32, shape, len(shape) - 2)


def _head_rms(x, g2):
    low = _lane_iota(x.shape) < HEAD_DIM
    ss = x * x
    s_lo = jnp.sum(jnp.where(low, ss, 0.0), axis=-1, keepdims=True)
    s_hi = jnp.sum(jnp.where(low, 0.0, ss), axis=-1, keepdims=True)
    r = jnp.where(low, lax.rsqrt(s_lo * (1.0 / HEAD_DIM) + EPS), lax.rsqrt(s_hi * (1.0 / HEAD_DIM) + EPS))
    return (x * r) * g2


def _rope(x, cos, sin_signed):
    w = x.shape[-1]
    half = HEAD_DIM // 2
    fwd = pltpu.roll(x, half, axis=1)
    bwd = pltpu.roll(x, w - half, axis=1)
    first = (_lane_iota(x.shape) % HEAD_DIM) < half
    return x * cos + jnp.where(first, bwd, fwd) * sin_signed


def _shifted_rows(u, carry, prevpad, seq8):
    t = u.shape[0]
    r1 = pltpu.roll(u, 1, axis=0)
    r2 = pltpu.roll(u, 2, axis=0)
    row = _row_iota(u.shape)
    if seq8:
        in_seq = row % SUBLANES
        r1 = jnp.where(in_seq == 0, pltpu.roll(prevpad, t - 1, axis=0), r1)
        r2 = jnp.where(in_seq < 2, prevpad, r2)
    else:
        c1 = jnp.broadcast_to(carry[SUBLANES - 1:SUBLANES, :], u.shape)
        c2 = jnp.broadcast_to(carry[SUBLANES - 2:SUBLANES - 1, :], u.shape)
        r1 = jnp.where(row == 0, c1, r1)
        r2 = jnp.where(row == 0, c2, jnp.where(row == 1, c1, r2))
    return r1, r2


def _proj_kernel(*refs, seq8):
    if seq8:
        (x_ref, g_ref, wa_ref, wb_ref, cw_ref, qg_ref, kg_ref, cos_ref, sin_ref, prev_ref,
         u_ref, tail_ref, kf_ref, vf_ref, kib_ref, qf_ref, qif_ref) = refs
        carry_ref = None
    else:
        (x_ref, g_ref, wa_ref, wb_ref, cw_ref, qg_ref, kg_ref, cos_ref, sin_ref,
         u_ref, tail_ref, kf_ref, vf_ref, kib_ref, qh_ref, qih_ref, kt_ref, kit_ref, vb_ref, carry_ref) = refs

    x = x_ref[0]
    xn = _rms(x, g_ref[...]).astype(jnp.bfloat16)

    def piece(p):
        return jnp.dot(xn, wa_ref[:, p * D_CONV:(p + 1) * D_CONV], preferred_element_type=jnp.float32)

    cb, cc, ch = piece(0), piece(1), piece(2)
    u = cc * ch
    if seq8:
        r1, r2 = _shifted_rows(u, None, prev_ref[0], True)
        tail_ref[0] = u
    else:
        @pl.when(pl.program_id(1) == 0)
        def _():
            carry_ref[...] = jnp.zeros_like(carry_ref)
        r1, r2 = _shifted_rows(u, carry_ref[...], None, False)
        tail = u[u.shape[0] - SUBLANES:, :]
        carry_ref[...] = tail
        tail_ref[0] = tail
    cw = cw_ref[...]
    conv = r2 * cw[0:1, :] + r1 * cw[1:2, :] + u * cw[2:3, :]
    u_ref[0] = (cb * conv).astype(u_ref.dtype)

    cos = cos_ref[...]
    sin = sin_ref[...]
    cos4 = jnp.concatenate([cos] * (D_ATT // LANES), axis=1)
    sin4 = jnp.concatenate([sin] * (D_ATT // LANES), axis=1)

    def normed(z, g_ref_):
        g2 = g_ref_[...]
        tiles = [_head_rms(z[:, c * LANES:(c + 1) * LANES], g2) for c in range(D_ATT // LANES)]
        return jnp.concatenate(tiles, axis=1)

    q = _rope(normed(piece(3), qg_ref), cos4, sin4) * Q_SCALE
    k = _rope(normed(piece(4), kg_ref), cos4, sin4)
    v = piece(5)
    qi = _rope(piece(6), cos4, sin4)
    zb = jnp.dot(xn, wb_ref[...], preferred_element_type=jnp.float32)
    lane = _lane_iota(zb.shape)
    kib = jnp.where(lane < IDX_DIM, _rope(zb, cos, sin), zb * IDX_SCALE)
    kf_ref[0] = k
    vf_ref[0] = v
    kib_ref[0] = kib
    if seq8:
        qf_ref[0] = q
        qif_ref[0] = qi
    else:
        for h in range(N_HEADS):
            qh_ref[0, h] = q[:, h * HEAD_DIM:(h + 1) * HEAD_DIM].astype(jnp.bfloat16)
            qih_ref[0, h] = qi[:, h * IDX_DIM:(h + 1) * IDX_DIM].astype(jnp.bfloat16)
        kt = k.T
        for h in range(N_HEADS):
            kt_ref[0, 0, h] = kt[h * HEAD_DIM:(h + 1) * HEAD_DIM, :].astype(jnp.bfloat16)
        kit_ref[0, 0] = kib.T[:IDX_DIM, :].astype(jnp.bfloat16)
        vb_ref[0] = v.astype(jnp.bfloat16)


def _rope_tables(pos):
    half = HEAD_DIM // 2
    inv = jnp.exp(jnp.arange(half, dtype=jnp.float32) * (-2.0 * math.log(ROPE_THETA) / HEAD_DIM))
    ang = pos.astype(jnp.float32)[:, None] * inv[None, :]
    cos = jnp.cos(ang)
    sin = jnp.sin(ang)
    cos_t = jnp.concatenate([cos, cos] * (LANES // HEAD_DIM), axis=1)
    sin_t = jnp.concatenate([-sin, sin] * (LANES // HEAD_DIM), axis=1)
    return cos_t, sin_t


def _const_spec(shape):
    nd = len(shape)
    return pl.BlockSpec(shape, lambda *_: (0,) * nd)


def _proj(x, prevpad, pos, g, wa, wb, cw, qg, kg, *, seq8):
    b, s, d = x.shape
    tm = TILE
    nt = s // tm
    cos_t, sin_t = _rope_tables(pos)
    row3 = lambda bi, i: (bi, i, 0)
    in_specs = [
        pl.BlockSpec((1, tm, d), row3),
        _const_spec((1, d)),
        _const_spec(wa.shape),
        _const_spec(wb.shape),
        _const_spec(cw.shape),
        _const_spec((1, LANES)),
        _const_spec((1, LANES)),
        pl.BlockSpec((tm, LANES), lambda bi, i: (i, 0)),
        pl.BlockSpec((tm, LANES), lambda bi, i: (i, 0)),
    ]
    args = [x, g, wa, wb, cw, qg, kg, cos_t, sin_t]
    f32 = jnp.float32
    out_shape = [
        jax.ShapeDtypeStruct((b, s, D_CONV), jnp.bfloat16),
    ]
    out_specs = [pl.BlockSpec((1, tm, D_CONV), row3)]
    if seq8:
        in_specs.append(pl.BlockSpec((1, tm, D_CONV), row3))
        args.append(prevpad)
        out_shape.append(jax.ShapeDtypeStruct((b, s, D_CONV), f32))
        out_specs.append(pl.BlockSpec((1, tm, D_CONV), row3))
    else:
        out_shape.append(jax.ShapeDtypeStruct((b, SUBLANES, D_CONV), f32))
        out_specs.append(pl.BlockSpec((1, SUBLANES, D_CONV), lambda bi, i: (bi, 0, 0)))
    out_shape += [
        jax.ShapeDtypeStruct((b, s, D_ATT), f32),
        jax.ShapeDtypeStruct((b, s, D_ATT), f32),
        jax.ShapeDtypeStruct((b, s, LANES), f32),
    ]
    out_specs += [pl.BlockSpec((1, tm, D_ATT), row3), pl.BlockSpec((1, tm, D_ATT), row3),
                  pl.BlockSpec((1, tm, LANES), row3)]
    scratch = []
    if seq8:
        out_shape += [jax.ShapeDtypeStruct((b, s, D_ATT), f32), jax.ShapeDtypeStruct((b, s, D_IDX), f32)]
        out_specs += [pl.BlockSpec((1, tm, D_ATT), row3), pl.BlockSpec((1, tm, D_IDX), row3)]
    else:
        bf = jnp.bfloat16
        out_shape += [
            jax.ShapeDtypeStruct((b, N_HEADS, s, HEAD_DIM), bf),
            jax.ShapeDtypeStruct((b, N_IDX_HEADS, s, IDX_DIM), bf),
            jax.ShapeDtypeStruct((b, nt, N_HEADS, HEAD_DIM, tm), bf),
            jax.ShapeDtypeStruct((b, nt, IDX_DIM, tm), bf),
            jax.ShapeDtypeStruct((b, s, D_ATT), bf),
        ]
        out_specs += [
            pl.BlockSpec((1, N_HEADS, tm, HEAD_DIM), lambda bi, i: (bi, 0, i, 0)),
            pl.BlockSpec((1, N_IDX_HEADS, tm, IDX_DIM), lambda bi, i: (bi, 0, i, 0)),
            pl.BlockSpec((1, 1, N_HEADS, HEAD_DIM, tm), lambda bi, i: (bi, i, 0, 0, 0)),
            pl.BlockSpec((1, 1, IDX_DIM, tm), lambda bi, i: (bi, i, 0, 0)),
            pl.BlockSpec((1, tm, D_ATT), row3),
        ]
        scratch = [pltpu.VMEM((SUBLANES, D_CONV), f32)]
    outs = pl.pallas_call(
        functools.partial(_proj_kernel, seq8=seq8),
        grid=(b, nt),
        in_specs=in_specs,
        out_specs=out_specs,
        out_shape=out_shape,
        scratch_shapes=scratch,
        compiler_params=pltpu.CompilerParams(
            dimension_semantics=("arbitrary", "arbitrary"), vmem_limit_bytes=VMEM_LIMIT),
        name="proj_sample" if seq8 else "proj_prompt",
    )(*args)
    return outs


NEG = -1e30
BIS_ROWS = 128


def _sortable(x):
    b = pltpu.bitcast(x, jnp.int32)
    return b ^ ((b >> 31) & 0x7FFFFFFF)


def _rep(x, n):
    return x if n == 1 else jnp.concatenate([x] * n, axis=1)


def _kth_largest_key(load_block, n_blocks, block_w, rows, k):
    assert k <= 2 * LANES
    tiles = block_w // LANES
    floor = jnp.full((rows, LANES), INT_MIN, jnp.int32)

    def top2(j, carry):
        m1, m2 = carry
        blk = load_block(j)
        for c in range(tiles):
            t = blk[:, c * LANES:(c + 1) * LANES]
            m2 = jnp.maximum(m2, jnp.minimum(m1, t))
            m1 = jnp.maximum(m1, t)
        return m1, m2

    m1, m2 = lax.fori_loop(0, n_blocks, top2, (floor, floor))
    lo0 = jnp.broadcast_to(jnp.min(m2, axis=-1, keepdims=True), (rows, LANES))
    hi0 = jnp.broadcast_to(jnp.max(m1, axis=-1, keepdims=True), (rows, LANES)) + 1

    def half_width(lo, hi):
        return lax.shift_right_logical(hi - lo, 1)

    def step(state):
        lo, hi, _ = state
        active = jnp.max(half_width(lo, hi))
        mid = lo + half_width(lo, hi)

        def count(j, cnt):
            blk = load_block(j)
            for c in range(tiles):
                cnt = cnt + jnp.where(blk[:, c * LANES:(c + 1) * LANES] >= mid, 1, 0)
            return cnt

        cnt = lax.fori_loop(0, n_blocks, count, jnp.zeros((rows, LANES), jnp.int32))
        total = jnp.sum(cnt, axis=-1, keepdims=True)
        lo = jnp.where(total >= k, mid, lo)
        hi = jnp.where(total > k, hi, jnp.where(total == k, mid + 1, mid))
        return lo, hi, active

    lo, _, _ = lax.while_loop(lambda st: st[2] > 0, step, (lo0, hi0, jnp.int32(1)))
    return lo


def _count_ge_gt(load_block, n_blocks, block_w, rows, v):
    def count(j, carry):
        ge, gt = carry
        blk = load_block(j)
        for c in range(block_w // LANES):
            t = blk[:, c * LANES:(c + 1) * LANES]
            ge = ge + jnp.where(t >= v, 1, 0)
            gt = gt + jnp.where(t > v, 1, 0)
        return ge, gt

    z = jnp.zeros((rows, LANES), jnp.int32)
    ge, gt = lax.fori_loop(0, n_blocks, count, (z, z))
    return jnp.sum(ge, axis=-1, keepdims=True), jnp.sum(gt, axis=-1, keepdims=True)


def _dsa_prompt_kernel(qh_ref, qih_ref, kib_ref, kt_ref, kit_ref, vb_ref, o_ref,
                       sc_ref, wb_ref, vs_ref, need_ref, tie_ref, bias_ref, m_ref, l_ref, acc_ref, *, topk):
    i = pl.program_id(1)
    tq = tk = TILE
    nl = tk // LANES
    f32 = jnp.float32

    for h in range(N_IDX_HEADS):
        wb_ref[h] = jnp.broadcast_to(kib_ref[0, :, IDX_DIM + h:IDX_DIM + h + 1], (tq, LANES))

    def score_block(j, diagonal):
        kit = kit_ref[0, j]
        acc = jnp.zeros((tq, tk), f32)
        for h in range(N_IDX_HEADS):
            d = jnp.dot(qih_ref[0, h], kit, preferred_element_type=f32)
            acc = acc + jnp.maximum(d, 0.0) * _rep(wb_ref[h], nl)
        key = _sortable(acc)
        if diagonal:
            key = jnp.where(_lane_iota(key.shape) > _row_iota(key.shape), INT_MIN, key)
        sc_ref[j] = key

    def score_body(j, c):
        score_block(j, False)
        return c

    lax.fori_loop(0, i, score_body, 0)
    score_block(i, True)

    @pl.when(i % 2 == 0)
    def _():
        sc_ref[i + 1] = jnp.full((tq, tk), INT_MIN, jnp.int32)

    n_pairs = (i + 2) // 2
    flag = jnp.int32(0)
    for rb in range(tq // BIS_ROWS):
        rows = slice(rb * BIS_ROWS, (rb + 1) * BIS_ROWS)
        load = lambda j: jnp.concatenate([sc_ref[2 * j, rows, :], sc_ref[2 * j + 1, rows, :]], axis=1)
        vstar = _kth_largest_key(load, n_pairs, 2 * tk, BIS_ROWS, topk)
        vs = jnp.maximum(vstar, INT_MIN + 1)
        n_ge, n_gt = _count_ge_gt(load, n_pairs, 2 * tk, BIS_ROWS, vs)
        vs_ref[rows, :] = vs
        need_ref[rows, :] = jnp.broadcast_to((topk - n_gt).astype(f32), (BIS_ROWS, LANES))
        flag = jnp.maximum(flag, jnp.max(jnp.where(n_ge > topk, 1, 0)))

    m_ref[...] = jnp.full(m_ref.shape, NEG, f32)
    l_ref[...] = jnp.zeros(l_ref.shape, f32)
    acc_ref[...] = jnp.zeros(acc_ref.shape, f32)
    tie_ref[...] = jnp.zeros(tie_ref.shape, f32)

    def attend(j, ties):
        key = sc_ref[j]
        vs = _rep(vs_ref[...], nl)
        if ties:
            eq = key == vs
            upper = (_row_iota((tk, tk)) <= _lane_iota((tk, tk))).astype(jnp.bfloat16)
            e = jnp.where(eq, 1.0, 0.0)
            rank = jnp.dot(e.astype(jnp.bfloat16), upper, preferred_element_type=f32) + _rep(tie_ref[...], nl)
            tie_ref[...] = tie_ref[...] + jnp.sum(e, axis=-1, keepdims=True)
            mask = (key > vs) | (eq & (rank <= _rep(need_ref[...], nl)))
        else:
            mask = key >= vs
        bias_ref[...] = jnp.where(mask, 0.0, NEG)
        for h in range(N_HEADS):
            sm = jnp.dot(qh_ref[0, h], kt_ref[0, j, h], preferred_element_type=f32) + bias_ref[...]
            m_old = m_ref[h]
            m_new = jnp.maximum(m_old, jnp.max(sm, axis=-1, keepdims=True))
            alpha = jnp.exp2(m_old - m_new)
            p = jnp.exp2(sm - _rep(m_new, nl))
            psum = p[:, :LANES]
            for c in range(1, nl):
                psum = psum + p[:, c * LANES:(c + 1) * LANES]
            l_ref[h] = alpha * l_ref[h] + psum
            lanes = slice((h // 2) * LANES, (h // 2 + 1) * LANES)
            pv = jnp.dot(p.astype(jnp.bfloat16), vb_ref[0, j, :, lanes], preferred_element_type=f32)
            acc_ref[h] = alpha * acc_ref[h] + pv
            m_ref[h] = m_new

    def run(ties):
        def body(jp, c):
            attend(2 * jp, ties)
            attend(2 * jp + 1, ties)
            return c
        lax.fori_loop(0, n_pairs, body, 0)

    @pl.when(flag == 0)
    def _():
        run(False)

    @pl.when(flag != 0)
    def _():
        run(True)

    low = _lane_iota((tq, LANES)) < HEAD_DIM
    for c in range(N_HEADS // 2):
        even = acc_ref[2 * c] / jnp.sum(l_ref[2 * c], axis=-1, keepdims=True)
        odd = acc_ref[2 * c + 1] / jnp.sum(l_ref[2 * c + 1], axis=-1, keepdims=True)
        o_ref[0, :, c * LANES:(c + 1) * LANES] = jnp.where(low, even, odd).astype(o_ref.dtype)


def _dsa_prompt(qh, qih, kib, kt, kit, vb):
    b, _, s, _ = qh.shape
    t = TILE
    nt = s // t
    assert nt % 2 == 0, "key blocks are counted in pairs"
    topk = min(TOPK_MAX, s // 4)
    vb4 = vb.reshape(b, nt, t, D_ATT)
    resident = dict(pipeline_mode=pl.Buffered(1))
    return pl.pallas_call(
        functools.partial(_dsa_prompt_kernel, topk=topk),
        grid=(b, nt),
        in_specs=[
            pl.BlockSpec((1, N_HEADS, t, HEAD_DIM), lambda bi, i: (bi, 0, i, 0)),
            pl.BlockSpec((1, N_IDX_HEADS, t, IDX_DIM), lambda bi, i: (bi, 0, i, 0)),
            pl.BlockSpec((1, t, LANES), lambda bi, i: (bi, i, 0)),
            pl.BlockSpec((1, nt, N_HEADS, HEAD_DIM, t), lambda bi, i: (bi, 0, 0, 0, 0), **resident),
            pl.BlockSpec((1, nt, IDX_DIM, t), lambda bi, i: (bi, 0, 0, 0), **resident),
            pl.BlockSpec((1, nt, t, D_ATT), lambda bi, i: (bi, 0, 0, 0), **resident),
        ],
        out_specs=pl.BlockSpec((1, t, D_ATT), lambda bi, i: (bi, i, 0)),
        out_shape=jax.ShapeDtypeStruct((b, s, D_ATT), jnp.bfloat16),
        scratch_shapes=[
            pltpu.VMEM((nt, t, t), jnp.int32),
            pltpu.VMEM((N_IDX_HEADS, t, LANES), jnp.float32),
            pltpu.VMEM((t, LANES), jnp.int32),
            pltpu.VMEM((t, LANES), jnp.float32),
            pltpu.VMEM((t, LANES), jnp.float32),
            pltpu.VMEM((t, t), jnp.float32),
            pltpu.VMEM((N_HEADS, t, LANES), jnp.float32),
            pltpu.VMEM((N_HEADS, t, LANES), jnp.float32),
            pltpu.VMEM((N_HEADS, t, LANES), jnp.float32),
        ],
        compiler_params=pltpu.CompilerParams(
            dimension_semantics=("arbitrary", "arbitrary"), vmem_limit_bytes=VMEM_LIMIT),
        name="dsa_prompt",
    )(qh, qih, kib, kt, kit, vb4)


def _mem_head_rms(x, g_ref):
    g = g_ref[...]
    heads = [_rms(x[:, h * MEM_HEAD_DIM:(h + 1) * MEM_HEAD_DIM], g) for h in range(MEM_HEADS)]
    return jnp.concatenate(heads, axis=1)


def _memkv_kernel(mem_ref, g_ref, w_ref, kg_ref, mk_ref, mv_ref):
    xn = _rms(mem_ref[0], g_ref[...]).astype(jnp.bfloat16)
    d = MEM_HEADS * MEM_HEAD_DIM
    mk = jnp.dot(xn, w_ref[:, :d], preferred_element_type=jnp.float32)
    mk_ref[0] = _mem_head_rms(mk, kg_ref)
    mv_ref[0] = jnp.dot(xn, w_ref[:, d:], preferred_element_type=jnp.float32)


def _memkv(mem, g, w_kv, kg):
    b, m, d = mem.shape
    dk = MEM_HEADS * MEM_HEAD_DIM
    blk = pl.BlockSpec((1, m, dk), lambda bi: (bi, 0, 0))
    return pl.pallas_call(
        _memkv_kernel,
        grid=(b,),
        in_specs=[pl.BlockSpec((1, m, d), lambda bi: (bi, 0, 0)), _const_spec((1, d)),
                  _const_spec(w_kv.shape), _const_spec((1, MEM_HEAD_DIM))],
        out_specs=[blk, blk],
        out_shape=[jax.ShapeDtypeStruct((b, m, dk), jnp.float32)] * 2,
        compiler_params=pltpu.CompilerParams(dimension_semantics=("arbitrary",), vmem_limit_bytes=VMEM_LIMIT),
        name="memory_kv",
    )(mem, g, w_kv, kg)


def _post_attn_kernel(x_ref, u_ref, att_ref, wo_ref, g_ref, wq_ref, qg_ref, h_ref, qm_ref):
    y = jnp.dot(u_ref[0].astype(jnp.bfloat16), wo_ref[:D_CONV, :], preferred_element_type=jnp.float32)
    y = y + jnp.dot(att_ref[0].astype(jnp.bfloat16), wo_ref[D_CONV:, :], preferred_element_type=jnp.float32)
    h = x_ref[0] + y
    h_ref[0] = h
    q = jnp.dot(_rms(h, g_ref[...]).astype(jnp.bfloat16), wq_ref[...], preferred_element_type=jnp.float32)
    qm_ref[0] = (_mem_head_rms(q, qg_ref) * MEM_SCALE).astype(qm_ref.dtype)


def _post_attn(x, u, att, wo, g, wq, qg, qm_dtype):
    b, s, d = x.shape
    tm = TILE
    row3 = lambda bi, i: (bi, i, 0)
    return pl.pallas_call(
        _post_attn_kernel,
        grid=(b, s // tm),
        in_specs=[pl.BlockSpec((1, tm, d), row3), pl.BlockSpec((1, tm, D_CONV), row3),
                  pl.BlockSpec((1, tm, D_ATT), row3), _const_spec(wo.shape), _const_spec((1, d)),
                  _const_spec(wq.shape), _const_spec((1, MEM_HEAD_DIM))],
        out_specs=[pl.BlockSpec((1, tm, d), row3), pl.BlockSpec((1, tm, d), row3)],
        out_shape=[jax.ShapeDtypeStruct((b, s, d), jnp.float32), jax.ShapeDtypeStruct((b, s, d), qm_dtype)],
        compiler_params=pltpu.CompilerParams(
            dimension_semantics=("arbitrary", "arbitrary"), vmem_limit_bytes=VMEM_LIMIT),
        name="post_attn",
    )(x, u, att, wo, g, wq, qg)


def _mem_attn_kernel(q_ref, mk_ref, mv_ref, o_ref):
    q = q_ref[0].astype(jnp.bfloat16)
    for h in range(MEM_HEADS):
        cols = slice(h * MEM_HEAD_DIM, (h + 1) * MEM_HEAD_DIM)
        logits = lax.dot_general(q[:, cols], mk_ref[0, :, cols].astype(jnp.bfloat16), NT_DIMS,
                                 preferred_element_type=jnp.float32)
        m = jnp.max(logits, axis=-1, keepdims=True)
        p = jnp.exp(logits - m)
        l = jnp.sum(p, axis=-1, keepdims=True)
        o = jnp.dot(p.astype(jnp.bfloat16), mv_ref[0, :, cols].astype(jnp.bfloat16),
                    preferred_element_type=jnp.float32)
        o_ref[0, :, cols] = (o / l).astype(o_ref.dtype)


def _mem_attn(qm, mk, mv, tm):
    g, s, d = qm.shape
    m = mk.shape[1]
    row3 = lambda gi, i: (gi, i, 0)
    grp = pl.BlockSpec((1, m, d), lambda gi, i: (gi, 0, 0))
    return pl.pallas_call(
        _mem_attn_kernel,
        grid=(g, s // tm),
        in_specs=[pl.BlockSpec((1, tm, d), row3), grp, grp],
        out_specs=pl.BlockSpec((1, tm, d), row3),
        out_shape=jax.ShapeDtypeStruct((g, s, d), qm.dtype),
        compiler_params=pltpu.CompilerParams(
            dimension_semantics=("arbitrary", "arbitrary"), vmem_limit_bytes=VMEM_LIMIT),
        name="mem_attn",
    )(qm, mk, mv)


def _ffn_kernel(*refs, seq8):
    if seq8:
        h_ref, o_ref, wo_ref, g_ref, wg_ref, wu_ref, cw_ref, cb_ref, wd_ref, prev_ref, y_ref, tail_ref = refs
        carry_ref = None
    else:
        h_ref, o_ref, wo_ref, g_ref, wg_ref, wu_ref, cw_ref, cb_ref, wd_ref, y_ref, tail_ref, carry_ref = refs
    h = h_ref[0] + jnp.dot(o_ref[0].astype(jnp.bfloat16), wo_ref[...], preferred_element_type=jnp.float32)
    xn = _rms(h, g_ref[...]).astype(jnp.bfloat16)
    gate = jnp.dot(xn, wg_ref[...], preferred_element_type=jnp.float32)
    up = jnp.dot(xn, wu_ref[...], preferred_element_type=jnp.float32)
    if seq8:
        r1, r2 = _shifted_rows(gate, None, prev_ref[0], True)
        tail_ref[0] = gate
    else:
        @pl.when(pl.program_id(1) == 0)
        def _():
            carry_ref[...] = jnp.zeros_like(carry_ref)
        r1, r2 = _shifted_rows(gate, carry_ref[...], None, False)
        tail = gate[gate.shape[0] - SUBLANES:, :]
        carry_ref[...] = tail
        tail_ref[0] = tail
    cw = cw_ref[...]
    gc = r2 * cw[0:1, :] + r1 * cw[1:2, :] + gate * cw[2:3, :] + cb_ref[...]
    act = (gc / (1.0 + jnp.exp(-gc))) * up
    y_ref[0] = h + jnp.dot(act.astype(jnp.bfloat16), wd_ref[...], preferred_element_type=jnp.float32)


def _ffn(h, o, prevpad, wo, g, wg, wu, cw, cb, wd, *, seq8):
    b, s, d = h.shape
    tm = TILE
    row3 = lambda bi, i: (bi, i, 0)
    resident = dict(pipeline_mode=pl.Buffered(1))
    wspec = lambda w: pl.BlockSpec(w.shape, lambda bi, i: (0, 0), **resident)
    in_specs = [pl.BlockSpec((1, tm, d), row3), pl.BlockSpec((1, tm, d), row3), wspec(wo), _const_spec((1, d)),
                wspec(wg), wspec(wu), _const_spec(cw.shape), _const_spec((1, D_FF)), wspec(wd)]
    args = [h, o, wo, g, wg, wu, cw, cb, wd]
    out_shape = [jax.ShapeDtypeStruct((b, s, d), jnp.float32)]
    out_specs = [pl.BlockSpec((1, tm, d), row3)]
    scratch = []
    if seq8:
        in_specs.append(pl.BlockSpec((1, tm, D_FF), row3))
        args.append(prevpad)
        out_shape.append(jax.ShapeDtypeStruct((b, s, D_FF), jnp.float32))
        out_specs.append(pl.BlockSpec((1, tm, D_FF), row3))
    else:
        out_shape.append(jax.ShapeDtypeStruct((b, SUBLANES, D_FF), jnp.float32))
        out_specs.append(pl.BlockSpec((1, SUBLANES, D_FF), lambda bi, i: (bi, 0, 0)))
        scratch = [pltpu.VMEM((SUBLANES, D_FF), jnp.float32)]
    return pl.pallas_call(
        functools.partial(_ffn_kernel, seq8=seq8),
        grid=(b, s // tm),
        in_specs=in_specs,
        out_specs=out_specs,
        out_shape=out_shape,
        scratch_shapes=scratch,
        compiler_params=pltpu.CompilerParams(
            dimension_semantics=("arbitrary", "arbitrary"), vmem_limit_bytes=VMEM_LIMIT),
        name="ffn_sample" if seq8 else "ffn_prompt",
    )(*args)


def _double_buffered_fetch(copies):
    b = pl.program_id(0)
    nb = pl.num_programs(0)
    slot = b % 2

    @pl.when(b == 0)
    def _():
        for cp in copies(b, slot):
            cp.start()

    @pl.when(b + 1 < nb)
    def _():
        for cp in copies(b + 1, 1 - slot):
            cp.start()

    for cp in copies(b, slot):
        cp.wait()
    return slot


def _paged_fetch(pt_ref, n_pages, pools, bufs, sems):
    def copies(seq, slot):
        return [pltpu.make_async_copy(pool.at[pt_ref[seq, p]], buf.at[slot, p], sem.at[slot])
                for pool, buf, sem in zip(pools, bufs, sems) for p in range(n_pages)]

    return _double_buffered_fetch(copies)


def _mem_attn_seq_kernel(q_ref, mk_hbm, mv_hbm, o_ref, kbuf_ref, vbuf_ref, ksem_ref, vsem_ref):
    def copies(seq, slot):
        return [pltpu.make_async_copy(src.at[seq, :, h, :], buf.at[slot, h], sem.at[slot])
                for src, buf, sem in ((mk_hbm, kbuf_ref, ksem_ref), (mv_hbm, vbuf_ref, vsem_ref))
                for h in range(MEM_HEADS)]

    slot = _double_buffered_fetch(copies)
    q = q_ref[0].astype(jnp.bfloat16)
    for h in range(MEM_HEADS):
        cols = slice(h * MEM_HEAD_DIM, (h + 1) * MEM_HEAD_DIM)
        logits = lax.dot_general(q[:, cols], kbuf_ref[slot, h].astype(jnp.bfloat16), NT_DIMS,
                                 preferred_element_type=jnp.float32)
        m = jnp.max(logits, axis=-1, keepdims=True)
        p = jnp.exp(logits - m)
        l = jnp.sum(p, axis=-1, keepdims=True)
        o = jnp.dot(p.astype(jnp.bfloat16), vbuf_ref[slot, h].astype(jnp.bfloat16),
                    preferred_element_type=jnp.float32)
        o_ref[0, :, cols] = o / l


def _mem_attn_seq(qm, mk, mv):
    g, t, d = qm.shape
    m = mk.shape[1]
    seq3 = lambda gi: (gi, 0, 0)
    buf = pltpu.VMEM((2, MEM_HEADS, m, MEM_HEAD_DIM), jnp.float32)
    return pl.pallas_call(
        _mem_attn_seq_kernel,
        grid=(g,),
        in_specs=[pl.BlockSpec((1, t, d), seq3), pl.BlockSpec(memory_space=pl.ANY),
                  pl.BlockSpec(memory_space=pl.ANY)],
        out_specs=pl.BlockSpec((1, t, d), seq3),
        out_shape=jax.ShapeDtypeStruct((g, t, d), jnp.float32),
        scratch_shapes=[buf, buf, pltpu.SemaphoreType.DMA((2,)), pltpu.SemaphoreType.DMA((2,))],
        compiler_params=pltpu.CompilerParams(dimension_semantics=("arbitrary",), vmem_limit_bytes=VMEM_LIMIT),
        name="mem_attn_seq",
    )(qm, mk, mv)


def _stack_heads(x, width):
    return jnp.concatenate([x[:, h * width:(h + 1) * width] for h in range(x.shape[1] // width)], axis=0)


def _pad_rows(x, rows):
    return jnp.concatenate([x, jnp.zeros((rows - x.shape[0], x.shape[1]), x.dtype)], axis=0)


NT_DIMS = (((1,), (1,)), ((), ()))
NN_DIMS = (((1,), (0,)), ((), ()))


def _idx_sample_kernel(pt_ref, qi_ref, kib_ref, pool_ref, sc_ref, buf_ref, sem_ref, *, n_pages):
    slot = _paged_fetch(pt_ref, n_pages, [pool_ref], [buf_ref], [sem_ref])
    f32 = jnp.float32
    ts = qi_ref.shape[1]
    kib = kib_ref[0]
    qis = _stack_heads(qi_ref[0], IDX_DIM).astype(jnp.bfloat16)
    w = [jnp.broadcast_to(kib[:, IDX_DIM + h:IDX_DIM + h + 1], (ts, LANES)) for h in range(N_IDX_HEADS)]

    def scores(keys, dims):
        d = lax.dot_general(qis, keys.astype(jnp.bfloat16), dims, preferred_element_type=f32)
        acc = jnp.zeros((ts, LANES), f32)
        for h in range(N_IDX_HEADS):
            acc = acc + jnp.maximum(d[h * ts:(h + 1) * ts, :], 0.0) * w[h]
        return _sortable(acc)

    for p in range(n_pages):
        sc_ref[:, p * PAGE_SIZE:(p + 1) * PAGE_SIZE] = scores(buf_ref[slot, p], NN_DIMS)
    new = scores(_pad_rows(kib[:, :IDX_DIM], LANES), NT_DIMS)
    causal = _lane_iota((ts, LANES)) <= _row_iota((ts, LANES))
    sc_ref[:, n_pages * PAGE_SIZE:] = jnp.where(causal, new, INT_MIN)


def _idx_sample(page_table, qif, kib, pool_idx):
    nb, n_pages = page_table.shape
    ts = qif.shape[0] // nb
    width = n_pages * PAGE_SIZE + LANES
    grid_spec = pltpu.PrefetchScalarGridSpec(
        num_scalar_prefetch=1,
        grid=(nb,),
        in_specs=[pl.BlockSpec((1, ts, D_IDX), lambda b, pt: (b, 0, 0)),
                  pl.BlockSpec((1, ts, LANES), lambda b, pt: (b, 0, 0)),
                  pl.BlockSpec(memory_space=pl.ANY)],
        out_specs=pl.BlockSpec((ts, width), lambda b, pt: (b, 0)),
        scratch_shapes=[pltpu.VMEM((2, n_pages, IDX_DIM, PAGE_SIZE), jnp.float32),
                        pltpu.SemaphoreType.DMA((2,))],
    )
    return pl.pallas_call(
        functools.partial(_idx_sample_kernel, n_pages=n_pages),
        grid_spec=grid_spec,
        out_shape=jax.ShapeDtypeStruct((nb * ts, width), jnp.int32),
        compiler_params=pltpu.CompilerParams(dimension_semantics=("arbitrary",), vmem_limit_bytes=VMEM_LIMIT),
        name="idx_sample",
    )(page_table, qif.reshape(nb, ts, D_IDX), kib.reshape(nb, ts, LANES), pool_idx)


def _threshold_kernel(sc_ref, vs_ref, need_ref, flag_ref, *, topk):
    rows, width = sc_ref.shape
    load = lambda j: sc_ref[...]
    vs = jnp.maximum(_kth_largest_key(load, 1, width, rows, topk), INT_MIN + 1)
    n_ge, n_gt = _count_ge_gt(load, 1, width, rows, vs)
    vs_ref[...] = vs
    need_ref[...] = jnp.broadcast_to((topk - n_gt).astype(jnp.float32), (rows, LANES))
    flag_ref[...] = jnp.broadcast_to(jnp.where(n_ge > topk, 1, 0), (rows, LANES))


def _threshold(sc, topk):
    n, width = sc.shape
    rows = BIS_ROWS
    blk = pl.BlockSpec((rows, LANES), lambda i: (i, 0))
    return pl.pallas_call(
        functools.partial(_threshold_kernel, topk=topk),
        grid=(n // rows,),
        in_specs=[pl.BlockSpec((rows, width), lambda i: (i, 0))],
        out_specs=[blk, blk, blk],
        out_shape=[jax.ShapeDtypeStruct((n, LANES), jnp.int32), jax.ShapeDtypeStruct((n, LANES), jnp.float32),
                   jax.ShapeDtypeStruct((n, LANES), jnp.int32)],
        compiler_params=pltpu.CompilerParams(dimension_semantics=("arbitrary",), vmem_limit_bytes=VMEM_LIMIT),
        name="threshold_sample",
    )(sc)


def _dsa_sample_kernel(pt_ref, q_ref, kn_ref, vn_ref, sc_ref, vs_ref, need_ref, flag_ref, kpool_ref, vpool_ref,
                       o_ref, kbuf_ref, vbuf_ref, ksem_ref, vsem_ref, lg_ref, *, n_pages):
    slot = _paged_fetch(pt_ref, n_pages, [kpool_ref, vpool_ref], [kbuf_ref, vbuf_ref], [ksem_ref, vsem_ref])
    f32 = jnp.float32
    bf16 = jnp.bfloat16
    ts = q_ref.shape[1]
    nblk = n_pages + 1
    q = q_ref[0]
    lane = _lane_iota(q.shape)
    head_of_lane = lane // HEAD_DIM
    qx = jnp.concatenate([jnp.where(head_of_lane == h, q, 0.0) for h in range(N_HEADS)], axis=0).astype(bf16)

    key = sc_ref[...]
    vs = vs_ref[...]
    has_ties = jnp.max(flag_ref[...]) > 0

    def tile_h(x):
        return jnp.concatenate([x] * N_HEADS, axis=0)

    def plain_mask(c):
        return jnp.where(key[:, c * LANES:(c + 1) * LANES] >= vs, 1, 0)

    def store_logits(masks):
        for c in range(nblk):
            if c < n_pages:
                s = jnp.dot(qx, kbuf_ref[slot, c].astype(bf16), preferred_element_type=f32)
            else:
                s = lax.dot_general(qx, _pad_rows(kn_ref[0], LANES).astype(bf16), NT_DIMS,
                                    preferred_element_type=f32)
            lg_ref[:, c * LANES:(c + 1) * LANES] = jnp.where(tile_h(masks(c)) > 0, s, NEG)

    @pl.when(jnp.logical_not(has_ties))
    def _():
        store_logits(plain_mask)

    @pl.when(has_ties)
    def _():
        upper = (_row_iota((LANES, LANES)) <= _lane_iota((LANES, LANES))).astype(bf16)
        need = need_ref[...]
        seen = jnp.zeros((ts, LANES), f32)
        masks = []
        for c in range(nblk):
            kc = key[:, c * LANES:(c + 1) * LANES]
            eq = kc == vs
            e = jnp.where(eq, 1.0, 0.0)
            rank = jnp.dot(e.astype(bf16), upper, preferred_element_type=f32) + seen
            seen = seen + jnp.sum(e, axis=-1, keepdims=True)
            masks.append(jnp.where((kc > vs) | (eq & (rank <= need)), 1, 0))
        store_logits(lambda c: masks[c])

    lg = lg_ref[...]
    m = jnp.max(lg, axis=-1, keepdims=True)
    p = jnp.exp2(lg - m)
    l = jnp.sum(p, axis=-1, keepdims=True)
    pb = p.astype(bf16)
    acc = jnp.zeros((N_HEADS * ts, D_ATT), f32)
    for c in range(nblk):
        pc = pb[:, c * LANES:(c + 1) * LANES]
        if c < n_pages:
            acc = acc + lax.dot_general(pc, vbuf_ref[slot, c].astype(bf16), NT_DIMS, preferred_element_type=f32)
        else:
            acc = acc + jnp.dot(pc, _pad_rows(vn_ref[0], LANES).astype(bf16), preferred_element_type=f32)
    acc = acc / l
    out = jnp.zeros((ts, D_ATT), f32)
    for h in range(N_HEADS):
        out = out + jnp.where(head_of_lane == h, acc[h * ts:(h + 1) * ts, :], 0.0)
    o_ref[0] = out


def _dsa_sample(page_table, qf, kf, vf, sc, vs, need, flag, pool_k, pool_v):
    nb, n_pages = page_table.shape
    ts = qf.shape[0] // nb
    width = sc.shape[1]
    seq3 = lambda b, pt: (b, 0, 0)
    row2 = lambda b, pt: (b, 0)
    grid_spec = pltpu.PrefetchScalarGridSpec(
        num_scalar_prefetch=1,
        grid=(nb,),
        in_specs=[pl.BlockSpec((1, ts, D_ATT), seq3), pl.BlockSpec((1, ts, D_ATT), seq3),
                  pl.BlockSpec((1, ts, D_ATT), seq3), pl.BlockSpec((ts, width), row2),
                  pl.BlockSpec((ts, LANES), row2), pl.BlockSpec((ts, LANES), row2), pl.BlockSpec((ts, LANES), row2),
                  pl.BlockSpec(memory_space=pl.ANY), pl.BlockSpec(memory_space=pl.ANY)],
        out_specs=pl.BlockSpec((1, ts, D_ATT), seq3),
        scratch_shapes=[pltpu.VMEM((2, n_pages, D_ATT, PAGE_SIZE), jnp.float32),
                        pltpu.VMEM((2, n_pages, D_ATT, PAGE_SIZE), jnp.float32),
                        pltpu.SemaphoreType.DMA((2,)), pltpu.SemaphoreType.DMA((2,)),
                        pltpu.VMEM((N_HEADS * ts, width), jnp.float32)],
    )
    r3 = lambda x: x.reshape(nb, ts, D_ATT)
    out = pl.pallas_call(
        functools.partial(_dsa_sample_kernel, n_pages=n_pages),
        grid_spec=grid_spec,
        out_shape=jax.ShapeDtypeStruct((nb, ts, D_ATT), jnp.float32),
        compiler_params=pltpu.CompilerParams(dimension_semantics=("arbitrary",), vmem_limit_bytes=VMEM_LIMIT),
        name="dsa_sample",
    )(page_table, r3(qf), r3(kf), r3(vf), sc, vs, need, flag, pool_k, pool_v)
    return out.reshape(nb * ts, D_ATT)


def _split_w_in(w_in):
    wide = N_WIDE * D_CONV
    wa = w_in[:, :wide].astype(jnp.bfloat16)
    wb = jnp.pad(w_in[:, wide:], ((0, 0), (0, LANES - (w_in.shape[1] - wide)))).astype(jnp.bfloat16)
    return wa, wb


def _tile2(g):
    return jnp.concatenate([g, g] * (LANES // (2 * HEAD_DIM)), axis=-1).reshape(1, LANES)


def kernel(x_prompt, x_sample, cache_k, cache_v, cache_idx_k, state_conv_mix, state_conv_ffn, cache_mem_k, cache_mem_v, page_table, mem_prompt, g_mix, w_in, conv_mix_w, q_norm_g, k_norm_g, w_out, g_mem, g_mem_src, w_q_mem, w_kv_mem, mq_norm_g, mk_norm_g, w_o_mem, g_ffn, w_gu, conv_ffn_w, conv_ffn_b, w_down):
    depth = w_in.shape[0]
    bp, s, _ = x_prompt.shape
    bs, ts, _ = x_sample.shape
    assert ts == SUBLANES, "the sample path treats each 8-row sublane group as one sequence"
    n_pages = page_table.shape[1]
    past = n_pages * PAGE_SIZE
    n_pool = cache_k.shape[1]
    ns = bs * ts
    bf16 = jnp.bfloat16
    row = lambda v: v.reshape(1, -1)
    pad_taps = lambda w: jnp.pad(w, ((0, SUBLANES - w.shape[0]), (0, 0)))

    def prevpad(state):
        return jnp.pad(state, ((0, 0), (0, ts - state.shape[1]), (0, 0))).reshape(1, ns, state.shape[2])

    def last2(tail, groups):
        c = tail.shape[-1]
        return tail.reshape(groups, SUBLANES, c)[:, SUBLANES - 2:, :]

    pos_p = jnp.arange(s)
    pos_s = jnp.tile(past + jnp.arange(ts), bs)
    topk_s = min(TOPK_MAX, (past + ts) // 4)
    hp = x_prompt
    hs = x_sample.reshape(1, ns, D_MODEL)
    outs = [[] for _ in range(12)]
    for l in range(depth):
        wa, wb = _split_w_in(w_in[l])
        g = row(g_mix[l])
        cw = pad_taps(conv_mix_w[l])
        qg, kg = _tile2(q_norm_g[l]), _tile2(k_norm_g[l])
        wo = w_out[l].astype(bf16)
        wq = w_q_mem[l].astype(bf16)
        wom = w_o_mem[l].astype(bf16)
        wg = w_gu[l][:, :D_FF].astype(bf16)
        wu = w_gu[l][:, D_FF:].astype(bf16)
        wd = w_down[l].astype(bf16)
        cwf = pad_taps(conv_ffn_w[l])
        cbf = row(conv_ffn_b[l])
        gm, gf, mqg = row(g_mem[l]), row(g_ffn[l]), row(mq_norm_g[l])

        up, tailp, kfp, vfp, kibp, qh, qih, kt, kit, vb = _proj(hp, None, pos_p, g, wa, wb, cw, qg, kg, seq8=False)
        attp = _dsa_prompt(qh, qih, kibp, kt, kit, vb)
        mk, mv = _memkv(mem_prompt, row(g_mem_src[l]), w_kv_mem[l].astype(bf16), row(mk_norm_g[l]))
        h1p, qmp = _post_attn(hp, up, attp, wo, gm, wq, mqg, bf16)
        omp = _mem_attn(qmp, mk, mv, TILE)
        hp, tailfp = _ffn(h1p, omp, None, wom, gf, wg, wu, cwf, cbf, wd, seq8=False)

        us, tails, kfs, vfs, kibs, qfs, qifs = _proj(
            hs, prevpad(state_conv_mix[l]), pos_s, g, wa, wb, cw, qg, kg, seq8=True)
        pool_ki = jnp.transpose(cache_idx_k[l], (0, 2, 1))
        pool_k = jnp.transpose(cache_k[l], (0, 2, 3, 1)).reshape(n_pool, D_ATT, PAGE_SIZE)
        pool_v = jnp.transpose(cache_v[l], (0, 2, 3, 1)).reshape(n_pool, D_ATT, PAGE_SIZE)
        sc = _idx_sample(page_table, qifs[0], kibs[0], pool_ki)
        vs, need, flag = _threshold(sc, topk_s)
        atts = _dsa_sample(page_table, qfs[0], kfs[0], vfs[0], sc, vs, need, flag,
                           pool_k, pool_v)
        h1s, qms = _post_attn(hs, us, atts.reshape(1, ns, D_ATT), wo, gm, wq, mqg, jnp.float32)
        oms = _mem_attn_seq(qms.reshape(bs, ts, D_MODEL), cache_mem_k[l], cache_mem_v[l])
        hs, tailfs = _ffn(h1s, oms.reshape(1, ns, D_MODEL), prevpad(state_conv_ffn[l]),
                          wom, gf, wg, wu, cwf, cbf, wd, seq8=True)

        per_layer = [
            kfp.reshape(bp, s, N_HEADS, HEAD_DIM), vfp.reshape(bp, s, N_HEADS, HEAD_DIM), kibp[:, :, :IDX_DIM],
            last2(tailp, bp), last2(tailfp, bp),
            mk.reshape(bp, -1, MEM_HEADS, MEM_HEAD_DIM), mv.reshape(bp, -1, MEM_HEADS, MEM_HEAD_DIM),
            kfs.reshape(bs, ts, N_HEADS, HEAD_DIM), vfs.reshape(bs, ts, N_HEADS, HEAD_DIM),
            kibs[0, :, :IDX_DIM].reshape(bs, ts, IDX_DIM),
            last2(tails, bs), last2(tailfs, bs),
        ]
        for acc, o in zip(outs, per_layer):
            acc.append(o)
    return (hp, hs.reshape(bs, ts, D_MODEL)) + tuple(jnp.stack(o) for o in outs)
```
